```python
import math
import jax, jax.numpy as jnp
from jax import lax
import numpy as np

D_MODEL = 4096
BATCH = 4
SEQ = 4096
DEPTH = 2

N_META = 16
N_A_LAYERS = DEPTH // 2
N_B_LAYERS = DEPTH - N_A_LAYERS
EPS = 1e-6

M_HEADS = 8
M_QK_DIM = D_MODEL // 2 // M_HEADS
M_V_DIM = D_MODEL // M_HEADS
M_QK_TOT = M_HEADS * M_QK_DIM
M_V_TOT = M_HEADS * M_V_DIM
M_CHUNK = 64
GATE_CAP = 15.0
M_IN_WIDTH = 2 * M_QK_TOT + 2 * M_V_TOT + 2 * M_HEADS
M_SPLITS = (M_QK_TOT, 2 * M_QK_TOT, 2 * M_QK_TOT + M_V_TOT,
            2 * M_QK_TOT + 2 * M_V_TOT, 2 * M_QK_TOT + 2 * M_V_TOT + M_HEADS)

A_HEADS = D_MODEL // 64
Q_LORA = D_MODEL // 4
KV_LORA = 512
NOPE_DIM = 128
ROPE_DIM = 64
V_HEAD_DIM = 128
QK_HEAD_DIM = NOPE_DIM + ROPE_DIM
ROPE_THETA = 10000.0
Q_BLOCK = 128

D_FF = -(-8 * D_MODEL // (3 * 256)) * 256

kernel_name = "yoco_mlstm_mla_hybrid"


def rms_norm(x, g):
    xf = x.astype(jnp.float32)
    y = xf * lax.rsqrt(jnp.mean(xf * xf, axis=-1, keepdims=True) + EPS)
    return (y * g.astype(jnp.float32)).astype(x.dtype)


def soft_cap(z):
    return GATE_CAP * jnp.tanh(z / GATE_CAP)


def rope_tables(pos):
    inv = 1.0 / (ROPE_THETA ** (jnp.arange(0, ROPE_DIM, 2, dtype=jnp.float32) / ROPE_DIM))
    ang = pos.astype(jnp.float32)[..., None] * inv
    return jnp.cos(ang), jnp.sin(ang)


def apply_rope(t, cos, sin):
    half = ROPE_DIM // 2
    t1, t2 = t[..., :half], t[..., half:]
    c, s = cos[:, :, None, :], sin[:, :, None, :]
    return jnp.concatenate([t1 * c - t2 * s, t1 * s + t2 * c], axis=-1).astype(t.dtype)


def mlstm_chunk(state, inp):
    c, n, m = state
    q, k, v, logi, logf = inp
    t_len = q.shape[2]
    b = jnp.cumsum(logf, axis=-1)
    causal = jnp.tril(jnp.ones((t_len, t_len), dtype=bool))
    d = jnp.where(causal, b[..., :, None] - b[..., None, :] + logi[..., None, :], -jnp.inf)
    inter = b + m[..., None]
    m_t = jnp.maximum(inter, jnp.max(d, axis=-1))
    w = jnp.exp(d - m_t[..., None])
    a = jnp.exp(inter - m_t)
    s = jnp.einsum('bhtd,bhsd->bhts', q, k) * w
    num = a[..., None] * jnp.einsum('bhtd,bhde->bhte', q, c) + jnp.einsum('bhts,bhse->bhte', s, v)
    den = a * jnp.einsum('bhtd,bhd->bht', q, n) + jnp.sum(s, axis=-1)
    h = num / jnp.maximum(jnp.abs(den), jnp.exp(-m_t))[..., None]
    b_last = b[..., -1]
    g = b_last[..., None] - b + logi
    m_new = jnp.maximum(b_last + m, jnp.max(g, axis=-1))
    decay = jnp.exp(b_last + m - m_new)
    kw = k * jnp.exp(g - m_new[..., None])[..., None]
    c_new = decay[..., None, None] * c + jnp.einsum('bhsd,bhse->bhde', kw, v)
    n_new = decay[..., None] * n + jnp.sum(kw, axis=2)
    return (c_new, n_new, m_new), h


def to_chunks(t):
    b_, h_, s_ = t.shape[:3]
    t = t.reshape((b_, h_, s_ // M_CHUNK, M_CHUNK) + t.shape[3:])
    return jnp.moveaxis(t, 2, 0)


def mlstm_mixer(xn, w_in, b_i, b_f, g_out, w_out):
    bsz, length, _ = xn.shape
    proj = xn @ w_in
    q, k, v, o, gi, gf = jnp.split(proj, M_SPLITS, axis=-1)

    def heads(t, dim):
        return t.reshape(bsz, length, M_HEADS, dim).transpose(0, 2, 1, 3).astype(jnp.float32)

    q = heads(q, M_QK_DIM)
    k = heads(k, M_QK_DIM) * (M_QK_DIM ** -0.5)
    v = heads(v, M_V_DIM)
    logi = soft_cap(gi.astype(jnp.float32) + b_i.astype(jnp.float32)).transpose(0, 2, 1)
    logf = jax.nn.log_sigmoid(soft_cap(gf.astype(jnp.float32) + b_f.astype(jnp.float32))).transpose(0, 2, 1)
    state0 = (jnp.zeros((bsz, M_HEADS, M_QK_DIM, M_V_DIM), jnp.float32),
              jnp.zeros((bsz, M_HEADS, M_QK_DIM), jnp.float32),
              jnp.zeros((bsz, M_HEADS), jnp.float32))
    seqs = (q, k, v, logi, logf)
    state, h_meta = mlstm_chunk(state0, tuple(t[:, :, :N_META] for t in seqs))
    _, h_rest = lax.scan(mlstm_chunk, state, tuple(to_chunks(t[:, :, N_META:]) for t in seqs))
    h_rest = jnp.moveaxis(h_rest, 0, 2).reshape(bsz, M_HEADS, length - N_META, M_V_DIM)
    hh = jnp.concatenate([h_meta, h_rest], axis=2).transpose(0, 2, 1, 3)
    hh = rms_norm(hh, g_out.reshape(M_HEADS, M_V_DIM))
    out = (hh.reshape(bsz, length, M_V_TOT) * jax.nn.sigmoid(o.astype(jnp.float32))).astype(xn.dtype)
    return out @ w_out


def swiglu(xn, w_gate_up, w_down):
    g, u = jnp.split(xn @ w_gate_up, 2, axis=-1)
    return (jax.nn.silu(g) * u) @ w_down


def shared_mla_kv(h, g_in, w_down, g_latent, w_up, g_k, cos, sin):
    bsz, length, _ = h.shape
    a = rms_norm(h, g_in) @ w_down
    c_kv, k_rope = a[..., :KV_LORA], a[..., KV_LORA:]
    kv = (rms_norm(c_kv, g_latent) @ w_up).reshape(bsz, length, A_HEADS, NOPE_DIM + V_HEAD_DIM)
    k_nope, v = kv[..., :NOPE_DIM], kv[..., NOPE_DIM:]
    k_rope = jnp.broadcast_to(k_rope[:, :, None, :], (bsz, length, A_HEADS, ROPE_DIM))
    k = rms_norm(jnp.concatenate([k_nope, k_rope], axis=-1), g_k)
    k = jnp.concatenate([k[..., :NOPE_DIM], apply_rope(k[..., NOPE_DIM:], cos, sin)], axis=-1)
    return k.transpose(0, 2, 1, 3), v.transpose(0, 2, 1, 3)


def causal_block_attention(q, k, v):
    length = q.shape[2]
    bounds = [0] + list(range(N_META, length, Q_BLOCK)) + [length]
    scale = QK_HEAD_DIM ** -0.5
    outs = []
    for lo, hi in zip(bounds[:-1], bounds[1:]):
        s = jnp.einsum('bhqd,bhkd->bhqk', q[:, :, lo:hi], k[:, :, :hi],
                       preferred_element_type=jnp.float32) * scale
        mask = (lo + jnp.arange(hi - lo))[:, None] >= jnp.arange(hi)[None, :]
        p = jax.nn.softmax(jnp.where(mask, s, -jnp.inf), axis=-1)
        outs.append(jnp.einsum('bhqk,bhkd->bhqd', p.astype(v.dtype), v[:, :, :hi]))
    return jnp.concatenate(outs, axis=2)


def mla_mixer(xn, k, v, w_dq, g_q_latent, w_uq, g_q, w_o, cos, sin):
    bsz, length, _ = xn.shape
    c_q = rms_norm(xn @ w_dq, g_q_latent)
    q = rms_norm((c_q @ w_uq).reshape(bsz, length, A_HEADS, QK_HEAD_DIM), g_q)
    q = jnp.concatenate([q[..., :NOPE_DIM], apply_rope(q[..., NOPE_DIM:], cos, sin)], axis=-1)
    o = causal_block_attention(q.transpose(0, 2, 1, 3), k, v)
    return o.transpose(0, 2, 1, 3).reshape(bsz, length, A_HEADS * V_HEAD_DIM) @ w_o


def setup_inputs(seed: int = 0) -> dict:
    key = jax.random.key(seed)
    ks = jax.random.split(key, 32)
    f32 = jnp.float32

    def nrm(k, shape, fan_in):
        return jax.random.normal(k, shape, f32) * (fan_in ** -0.5)

    def gain(k, shape):
        return 1.0 + 0.02 * jax.random.normal(k, shape, f32)

    x = jax.random.normal(ks[0], (BATCH, SEQ, D_MODEL), f32)
    offsets = jax.random.randint(ks[1], (BATCH, 1), 0, 256, dtype=jnp.int32)
    positions = (offsets + jnp.arange(SEQ, dtype=jnp.int32)[None, :]).astype(jnp.int32)
    meta_tokens = jax.random.normal(ks[2], (N_META, D_MODEL), f32)
    a_b_f = (jnp.broadcast_to(jnp.linspace(3.0, 6.0, M_HEADS, dtype=f32), (N_A_LAYERS, M_HEADS))
             + 0.1 * jax.random.normal(ks[6], (N_A_LAYERS, M_HEADS), f32))
    return {
        "x": x,
        "positions": positions,
        "meta_tokens": meta_tokens,
        "a_norm_g": gain(ks[3], (N_A_LAYERS, D_MODEL)),
        "a_w_in": nrm(ks[4], (N_A_LAYERS, D_MODEL, M_IN_WIDTH), D_MODEL),
        "a_b_i": 0.1 * jax.random.normal(ks[5], (N_A_LAYERS, M_HEADS), f32),
        "a_b_f": a_b_f,
        "a_out_norm_g": gain(ks[7], (N_A_LAYERS, M_V_TOT)),
        "a_w_out": nrm(ks[8], (N_A_LAYERS, M_V_TOT, D_MODEL), M_V_TOT),
        "ffn_norm_g": gain(ks[9], (DEPTH, D_MODEL)),
        "ffn_w_gate_up": nrm(ks[10], (DEPTH, D_MODEL, 2 * D_FF), D_MODEL),
        "ffn_w_down": nrm(ks[11], (DEPTH, D_FF, D_MODEL), D_FF),
        "kv_norm_g": gain(ks[12], (D_MODEL,)),
        "kv_w_down": nrm(ks[13], (D_MODEL, KV_LORA + ROPE_DIM), D_MODEL),
        "kv_latent_norm_g": gain(ks[14], (KV_LORA,)),
        "kv_w_up": nrm(ks[15], (KV_LORA, A_HEADS * (NOPE_DIM + V_HEAD_DIM)), KV_LORA),
        "k_norm_g": gain(ks[16], (QK_HEAD_DIM,)),
        "b_norm_g": gain(ks[17], (N_B_LAYERS, D_MODEL)),
        "b_w_dq": nrm(ks[18], (N_B_LAYERS, D_MODEL, Q_LORA), D_MODEL),
        "b_q_latent_norm_g": gain(ks[19], (N_B_LAYERS, Q_LORA)),
        "b_w_uq": nrm(ks[20], (N_B_LAYERS, Q_LORA, A_HEADS * QK_HEAD_DIM), Q_LORA),
        "q_norm_g": gain(ks[21], (N_B_LAYERS, QK_HEAD_DIM)),
        "b_w_o": nrm(ks[22], (N_B_LAYERS, A_HEADS * V_HEAD_DIM, D_MODEL), A_HEADS * V_HEAD_DIM),
    }


def reference(x, positions, meta_tokens, a_norm_g, a_w_in, a_b_i, a_b_f, a_out_norm_g, a_w_out,
              ffn_norm_g, ffn_w_gate_up, ffn_w_down, kv_norm_g, kv_w_down, kv_latent_norm_g,
              kv_w_up, k_norm_g, b_norm_g, b_w_dq, b_q_latent_norm_g, b_w_uq, q_norm_g, b_w_o):
    bsz = x.shape[0]
    meta = jnp.broadcast_to(meta_tokens[None].astype(x.dtype), (bsz, N_META, D_MODEL))
    h = jnp.concatenate([meta, x], axis=1)
    meta_pos = jnp.broadcast_to(jnp.arange(N_META, dtype=jnp.int32)[None, :], (bsz, N_META))
    pos = jnp.concatenate([meta_pos, positions.astype(jnp.int32) + N_META], axis=1)
    cos, sin = rope_tables(pos)
    k_shared, v_shared = None, None
    for layer in range(DEPTH):
        if layer < N_A_LAYERS:
            i = layer
            h = h + mlstm_mixer(rms_norm(h, a_norm_g[i]), a_w_in[i], a_b_i[i], a_b_f[i],
                                a_out_norm_g[i], a_w_out[i])
        else:
            j = layer - N_A_LAYERS
            if j == 0:
                k_shared, v_shared = shared_mla_kv(h, kv_norm_g, kv_w_down, kv_latent_norm_g,
                                                   kv_w_up, k_norm_g, cos, sin)
            h = h + mla_mixer(rms_norm(h, b_norm_g[j]), k_shared, v_shared, b_w_dq[j],
                              b_q_latent_norm_g[j], b_w_uq[j], q_norm_g[j], b_w_o[j], cos, sin)
        h = h + swiglu(rms_norm(h, ffn_norm_g[layer]), ffn_w_gate_up[layer], ffn_w_down[layer])
    return h[:, N_META:]
```

```python
import functools

import jax
import jax.numpy as jnp
from jax import lax
from jax.experimental import pallas as pl
from jax.experimental.pallas import tpu as pltpu

F32 = jnp.float32
BF16 = jnp.bfloat16

EPS = 1e-6
MLSTM_HEADS = 8
GATE_CAP = 15.0
NOPE_DIM = 128
ROPE_DIM = 64
V_HEAD_DIM = 128
QK_HEAD_DIM = NOPE_DIM + ROPE_DIM
HEAD_SLOT = 2 * NOPE_DIM
ROPE_THETA = 10000.0
MLSTM_CHUNK = 256
NORM_ROWS = 16

V7X_VMEM_BYTES = 64 * 1024 * 1024
VMEM_MARGIN_BYTES = 8 * 1024 * 1024
VMEM_CAP_BYTES = 58 * 1024 * 1024

NT_DIMS = (((1,), (1,)), ((), ()))
TN_DIMS = (((0,), (0,)), ((), ()))


def _params(semantics, buffer_bytes):
    limit = min(int(buffer_bytes) + VMEM_MARGIN_BYTES, VMEM_CAP_BYTES)
    return pltpu.CompilerParams(dimension_semantics=semantics, vmem_limit_bytes=limit)


def _tile(n, pref, align):
    if n <= pref:
        return n
    t = (pref // align) * align
    while t >= align:
        if n % t == 0:
            return t
        t -= align
    return n


def _rms_to_bf16(h_ref, gains, outs):
    tm = h_ref.shape[0]
    rc = NORM_ROWS if tm % NORM_ROWS == 0 else tm

    def body(c, carry):
        r0 = pl.multiple_of(c * rc, rc)
        x = h_ref[pl.ds(r0, rc), :]
        y = x * lax.rsqrt(jnp.mean(x * x, axis=-1, keepdims=True) + EPS)
        for g, o_ref in zip(gains, outs):
            o_ref[pl.ds(r0, rc), :] = (y * g).astype(BF16)
        return carry

    lax.fori_loop(0, tm // rc, body, 0)


def _sigmoid(x):
    return 1.0 / (1.0 + jnp.exp(-x))


def _in_proj_kernel(h_ref, g_ref, w_ref, wgt_ref, bcol_ref, o_ref, gt_ref, xn_ref):
    @pl.when(pl.program_id(1) == 0)
    def _():
        _rms_to_bf16(h_ref, [g_ref[...]], [xn_ref])
        z = lax.dot_general(wgt_ref[...], xn_ref[...], NT_DIMS, preferred_element_type=F32)
        z = z + bcol_ref[...]
        capped = GATE_CAP * jnp.tanh(z / GATE_CAP)
        logf = jnp.minimum(capped, 0.0) - jnp.log(1.0 + jnp.exp(-jnp.abs(capped)))
        row = lax.broadcasted_iota(jnp.int32, z.shape, 0)
        gt_ref[...] = jnp.where(row < MLSTM_HEADS, capped, logf)

    o_ref[...] = jnp.dot(xn_ref[...], w_ref[...], preferred_element_type=F32).astype(o_ref.dtype)


def _in_proj(h, g, w_main, wg_t, b_col):
    m, d = h.shape
    n = w_main.shape[1]
    tm = _tile(m, 1024, 16)
    tn = _tile(n, 1024, 128)
    ng = wg_t.shape[0]
    single = tm * d * 4 > 8 * 1024 * 1024
    h_spec = pl.BlockSpec((tm, d), lambda i, j: (i, 0),
                          pipeline_mode=pl.Buffered(1) if single else None)
    buf = (tm * d * 4 * (1 if single else 2) + tm * d * 2 + 2 * d * tn * 2 + 2 * tm * tn * 2
           + 2 * ng * d * 2 + 2 * ng * tm * 4 + tm * tn * 4)
    return pl.pallas_call(
        _in_proj_kernel,
        grid=(m // tm, n // tn),
        in_specs=[h_spec,
                  pl.BlockSpec((1, d), lambda i, j: (0, 0)),
                  pl.BlockSpec((d, tn), lambda i, j: (0, j)),
                  pl.BlockSpec((ng, d), lambda i, j: (0, 0)),
                  pl.BlockSpec((ng, 1), lambda i, j: (0, 0))],
        out_specs=[pl.BlockSpec((tm, tn), lambda i, j: (i, j)),
                   pl.BlockSpec((ng, tm), lambda i, j: (0, i))],
        out_shape=[jax.ShapeDtypeStruct((m, n), BF16),
                   jax.ShapeDtypeStruct((ng, m), F32)],
        scratch_shapes=[pltpu.VMEM((tm, d), BF16)],
        compiler_params=_params(("parallel", "arbitrary"), buf),
        name="mlstm_in_proj",
    )(h, g, w_main, wg_t, b_col)


def _mlstm_kernel(q_ref, k_ref, v_ref, og_ref, gc_ref, gr_ref, gout_ref, c0_ref, n0_ref, m0_ref,
                  o_ref, ct_ref, nt_ref, mt_ref, c_sc, n_sc, m_sc, *, k_scale):
    hd = pl.program_id(1)
    ci = pl.program_id(2)
    t = q_ref.shape[0]

    @pl.when(ci == 0)
    def _():
        c_sc[...] = c0_ref[0, 0]
        n_sc[...] = n0_ref[0, 0]
        m_sc[...] = m0_ref[0, 0]

    q = q_ref[...]
    k = k_ref[...] * k_scale
    v = v_ref[...]
    gc = gc_ref[0]
    gr = gr_ref[0]
    lane = lax.broadcasted_iota(jnp.int32, gc.shape, 1)
    sub = lax.broadcasted_iota(jnp.int32, gr.shape, 0)
    logi_c = jnp.sum(jnp.where(lane == hd, gc, 0.0), axis=1, keepdims=True)
    logf_c = jnp.sum(jnp.where(lane == hd + MLSTM_HEADS, gc, 0.0), axis=1, keepdims=True)
    logi_r = jnp.sum(jnp.where(sub == hd, gr, 0.0), axis=0, keepdims=True)
    logf_r = jnp.sum(jnp.where(sub == hd + MLSTM_HEADS, gr, 0.0), axis=0, keepdims=True)

    row = lax.broadcasted_iota(jnp.int32, (t, t), 0)
    col = lax.broadcasted_iota(jnp.int32, (t, t), 1)
    causal = row >= col
    b_c = jnp.sum(jnp.where(causal, logf_r, 0.0), axis=1, keepdims=True)
    b_r = jnp.sum(jnp.where(causal, 0.0, logf_c), axis=0, keepdims=True) + logf_r
    m_prev = m_sc[0:1, 0:1]
    c_prev = c_sc[...]
    n_prev = n_sc[...]

    d = jnp.where(causal, b_c - b_r + logi_r, -jnp.inf)
    inter = b_c + m_prev
    m_t = jnp.maximum(inter, jnp.max(d, axis=1, keepdims=True))
    w = jnp.exp(d - m_t)
    a = jnp.exp(inter - m_t)
    s = lax.dot_general(q, k, NT_DIMS, preferred_element_type=F32) * w
    num = (a * jnp.dot(q, c_prev.astype(BF16), preferred_element_type=F32)
           + jnp.dot(s.astype(BF16), v, preferred_element_type=F32))
    den = (a * jnp.sum(q.astype(F32) * n_prev, axis=1, keepdims=True)
           + jnp.sum(s, axis=1, keepdims=True))
    hval = num * (1.0 / jnp.maximum(jnp.abs(den), jnp.exp(-m_t)))

    ms = jnp.mean(hval * hval, axis=1, keepdims=True)
    hn = hval * lax.rsqrt(ms + EPS) * gout_ref[...]
    o_ref[...] = (hn * _sigmoid(og_ref[...].astype(F32))).astype(o_ref.dtype)

    b_last = jnp.sum(logf_r, axis=1, keepdims=True)
    g_c = b_last - b_c + logi_c
    m_new = jnp.maximum(b_last + m_prev, jnp.max(g_c, axis=0, keepdims=True))
    decay = jnp.exp(b_last + m_prev - m_new)
    kw = k.astype(F32) * jnp.exp(g_c - m_new)
    c_new = decay * c_prev + lax.dot_general(kw.astype(BF16), v, TN_DIMS, preferred_element_type=F32)
    n_new = decay * n_prev + jnp.sum(kw, axis=0, keepdims=True)
    c_sc[...] = c_new
    n_sc[...] = n_new
    m_sc[...] = jnp.broadcast_to(m_new, m_sc.shape)

    @pl.when(ci == pl.num_programs(2) - 1)
    def _():
        ct_ref[0, 0] = c_new
        nt_ref[0, 0] = n_new
        mt_ref[0, 0] = jnp.broadcast_to(m_new, m_sc.shape)


def _mlstm(proj, gates_t, g_out, state, *, bsz, chunk):
    m = proj.shape[0]
    c0, n0, m0 = state
    dk, dv = c0.shape[2], c0.shape[3]
    nh = MLSTM_HEADS
    nc = m // (bsz * chunk)
    g_row = gates_t.reshape(2 * nh, bsz * nc, chunk).transpose(1, 0, 2)
    g_col = g_row.transpose(0, 2, 1)
    kq = nh
    kv = (2 * nh * dk) // dv
    ko = kv + nh
    rowblk = lambda b, h, c: b * nc + c
    st_spec = lambda shp: pl.BlockSpec((1, 1) + shp, lambda b, h, c: (b, h, 0, 0))
    buf = 2 * (2 * chunk * dk * 2 + 2 * chunk * dv * 2 + chunk * dv * 2) + 8 * dk * dv * 4 + (1 << 20)
    out, ct, nt, mt = pl.pallas_call(
        functools.partial(_mlstm_kernel, k_scale=dk ** -0.5),
        grid=(bsz, nh, nc),
        in_specs=[pl.BlockSpec((chunk, dk), lambda b, h, c: (rowblk(b, h, c), h)),
                  pl.BlockSpec((chunk, dk), lambda b, h, c: (rowblk(b, h, c), kq + h)),
                  pl.BlockSpec((chunk, dv), lambda b, h, c: (rowblk(b, h, c), kv + h)),
                  pl.BlockSpec((chunk, dv), lambda b, h, c: (rowblk(b, h, c), ko + h)),
                  pl.BlockSpec((1, chunk, 2 * nh), lambda b, h, c: (rowblk(b, h, c), 0, 0)),
                  pl.BlockSpec((1, 2 * nh, chunk), lambda b, h, c: (rowblk(b, h, c), 0, 0)),
                  pl.BlockSpec((1, dv), lambda b, h, c: (0, h)),
                  st_spec((dk, dv)), st_spec((1, dk)), st_spec((8, 128))],
        out_specs=[pl.BlockSpec((chunk, dv), lambda b, h, c: (rowblk(b, h, c), h)),
                   st_spec((dk, dv)), st_spec((1, dk)), st_spec((8, 128))],
        out_shape=[jax.ShapeDtypeStruct((m, nh * dv), BF16),
                   jax.ShapeDtypeStruct(c0.shape, F32),
                   jax.ShapeDtypeStruct(n0.shape, F32),
                   jax.ShapeDtypeStruct(m0.shape, F32)],
        scratch_shapes=[pltpu.VMEM((dk, dv), F32), pltpu.VMEM((1, dk), F32), pltpu.VMEM((8, 128), F32)],
        compiler_params=_params(("parallel", "parallel", "arbitrary"), buf),
        name="mlstm_chunks",
    )(proj, proj, proj, proj, g_col, g_row, g_out, c0, n0, m0)
    return out, (ct, nt, mt)


def _mm_resid_kernel(x_ref, w_ref, r_ref, o_ref):
    o_ref[...] = r_ref[...] + jnp.dot(x_ref[...], w_ref[...], preferred_element_type=F32)


def _mm_resid(x, w, resid, *, tm_pref, tn_pref, name):
    m, k = x.shape
    n = w.shape[1]
    tm = _tile(m, tm_pref, 16)
    tn = _tile(n, tn_pref, 128)
    buf = 2 * (tm * k * 2 + k * tn * 2 + 2 * tm * tn * 4) + tm * tn * 4
    return pl.pallas_call(
        _mm_resid_kernel,
        grid=(m // tm, n // tn),
        in_specs=[pl.BlockSpec((tm, k), lambda i, j: (i, 0)),
                  pl.BlockSpec((k, tn), lambda i, j: (0, j)),
                  pl.BlockSpec((tm, tn), lambda i, j: (i, j))],
        out_specs=pl.BlockSpec((tm, tn), lambda i, j: (i, j)),
        out_shape=jax.ShapeDtypeStruct((m, n), F32),
        compiler_params=_params(("parallel", "parallel"), buf),
        name=name,
    )(x, w, resid)


def _ffn_up_kernel(h_ref, g_ref, wg_ref, wu_ref, o_ref, xn_ref):
    @pl.when(pl.program_id(1) == 0)
    def _():
        _rms_to_bf16(h_ref, [g_ref[...]], [xn_ref])

    x = xn_ref[...]
    gate = jnp.dot(x, wg_ref[...], preferred_element_type=F32)
    up = jnp.dot(x, wu_ref[...], preferred_element_type=F32)
    o_ref[...] = (gate * _sigmoid(gate) * up).astype(o_ref.dtype)


def _ffn_up(h, g, w_gate_up):
    m, d = h.shape
    f = w_gate_up.shape[1] // 2
    tm = _tile(m, 1024, 16)
    tn = _tile(f, 256, 128)
    nj = f // tn
    single = tm * d * 4 > 8 * 1024 * 1024
    h_spec = pl.BlockSpec((tm, d), lambda i, j: (i, 0),
                          pipeline_mode=pl.Buffered(1) if single else None)
    buf = (tm * d * 4 * (1 if single else 2) + tm * d * 2 + 4 * d * tn * 2 + 2 * tm * tn * 2
           + 2 * tm * tn * 4)
    return pl.pallas_call(
        _ffn_up_kernel,
        grid=(m // tm, nj),
        in_specs=[h_spec,
                  pl.BlockSpec((1, d), lambda i, j: (0, 0)),
                  pl.BlockSpec((d, tn), lambda i, j: (0, j)),
                  pl.BlockSpec((d, tn), lambda i, j: (0, j + nj))],
        out_specs=pl.BlockSpec((tm, tn), lambda i, j: (i, j)),
        out_shape=jax.ShapeDtypeStruct((m, f), BF16),
        scratch_shapes=[pltpu.VMEM((tm, d), BF16)],
        compiler_params=_params(("parallel", "arbitrary"), buf),
        name="ffn_up",
    )(h, g, w_gate_up, w_gate_up)


def _kvq_down_kernel(*refs, with_q, kv_lora):
    if with_q:
        (h_ref, gkv_ref, gb_ref, wkv_ref, wdq_ref, glat_ref, gql_ref,
         ckv_ref, kr_ref, cq_ref, xa_ref, xb_ref) = refs
        _rms_to_bf16(h_ref, [gkv_ref[...], gb_ref[...]], [xa_ref, xb_ref])
    else:
        h_ref, gkv_ref, wkv_ref, glat_ref, ckv_ref, kr_ref, xa_ref = refs
        _rms_to_bf16(h_ref, [gkv_ref[...]], [xa_ref])

    a = jnp.dot(xa_ref[...], wkv_ref[...], preferred_element_type=F32)
    ckv = a[:, :kv_lora]
    ckv_ref[...] = (ckv * lax.rsqrt(jnp.mean(ckv * ckv, axis=-1, keepdims=True) + EPS)
                    * glat_ref[...]).astype(ckv_ref.dtype)
    kr_ref[...] = a[:, kv_lora:]
    if with_q:
        cq = jnp.dot(xb_ref[...], wdq_ref[...], preferred_element_type=F32)
        cq_ref[...] = (cq * lax.rsqrt(jnp.mean(cq * cq, axis=-1, keepdims=True) + EPS)
                       * gql_ref[...]).astype(cq_ref.dtype)


def _kvq_down(h, g_kv, w_kv, g_lat, q_parts=None):
    m, d = h.shape
    kv_lora = g_lat.shape[1]
    nkv = w_kv.shape[1]
    tm = _tile(m, 256, 16)
    row = lambda shp: pl.BlockSpec(shp, lambda i: (i, 0))
    full = lambda shp: pl.BlockSpec(shp, lambda i: (0, 0))
    with_q = q_parts is not None
    if with_q:
        g_b, w_dq, g_ql = q_parts
        nq = w_dq.shape[1]
        args = (h, g_kv, g_b, w_kv, w_dq, g_lat, g_ql)
        in_specs = [row((tm, d)), full((1, d)), full((1, d)), full((d, nkv)), full((d, nq)),
                    full((1, kv_lora)), full((1, nq))]
        out_specs = [row((tm, kv_lora)), row((tm, nkv - kv_lora)), row((tm, nq))]
        out_shape = [jax.ShapeDtypeStruct((m, kv_lora), BF16),
                     jax.ShapeDtypeStruct((m, nkv - kv_lora), F32),
                     jax.ShapeDtypeStruct((m, nq), BF16)]
        scratch = [pltpu.VMEM((tm, d), BF16), pltpu.VMEM((tm, d), BF16)]
        buf = 2 * tm * d * 4 + 2 * tm * d * 2 + 2 * d * (nkv + nq) * 2 + 2 * tm * (nkv + nq) * 4
    else:
        args = (h, g_kv, w_kv, g_lat)
        in_specs = [row((tm, d)), full((1, d)), full((d, nkv)), full((1, kv_lora))]
        out_specs = [row((tm, kv_lora)), row((tm, nkv - kv_lora))]
        out_shape = [jax.ShapeDtypeStruct((m, kv_lora), BF16),
                     jax.ShapeDtypeStruct((m, nkv - kv_lora), F32)]
        scratch = [pltpu.VMEM((tm, d), BF16)]
        buf = 2 * tm * d * 4 + tm * d * 2 + 2 * d * nkv * 2 + 2 * tm * nkv * 4
    return pl.pallas_call(
        functools.partial(_kvq_down_kernel, with_q=with_q, kv_lora=kv_lora),
        grid=(m // tm,),
        in_specs=in_specs, out_specs=out_specs, out_shape=out_shape, scratch_shapes=scratch,
        compiler_params=_params(("parallel",), buf),
        name="mla_down_q" if with_q else "mla_down_kv",
    )(*args)


def _fold_rope(y):
    return y + pltpu.roll(y, ROPE_DIM, 1)


def _kv_up_kernel(c_ref, kr_ref, tab_ref, w_ref, gk_ref, k_ref, v_ref, *, hb):
    c = c_ref[...]
    kr = kr_ref[...]
    gk = gk_ref[...]
    lane = lax.broadcasted_iota(jnp.int32, kr.shape, 1)
    low = lane < ROPE_DIM
    ssq_rope = jnp.sum(jnp.where(low, kr * kr, 0.0), axis=1, keepdims=True)
    pt = kr * gk[:, NOPE_DIM:] * tab_ref[...]
    for hd in range(hb):
        kvv = jnp.dot(c, w_ref[:, hd * HEAD_SLOT:(hd + 1) * HEAD_SLOT], preferred_element_type=F32)
        kn = kvv[:, :NOPE_DIM]
        r = lax.rsqrt((jnp.sum(kn * kn, axis=1, keepdims=True) + ssq_rope) * (1.0 / QK_HEAD_DIM) + EPS)
        k_ref[hd, :, 0:NOPE_DIM] = (kn * r * gk[:, :NOPE_DIM]).astype(k_ref.dtype)
        k_ref[hd, :, NOPE_DIM:] = jnp.where(low, _fold_rope(pt * r), 0.0).astype(k_ref.dtype)
        v_ref[hd] = kvv[:, NOPE_DIM:].astype(v_ref.dtype)


def _kv_up(ckv, kr, tab, w_up, gk_row, *, heads):
    m, kl = ckv.shape
    hb = 2
    tm = _tile(m, 1024, 16)
    buf = 2 * (tm * kl * 2 + 2 * tm * 128 * 4 + kl * hb * HEAD_SLOT * 2
               + hb * tm * HEAD_SLOT * 2 + hb * tm * V_HEAD_DIM * 2) + 4 * tm * HEAD_SLOT * 4
    return pl.pallas_call(
        functools.partial(_kv_up_kernel, hb=hb),
        grid=(m // tm, heads // hb),
        in_specs=[pl.BlockSpec((tm, kl), lambda i, j: (i, 0)),
                  pl.BlockSpec((tm, 2 * ROPE_DIM), lambda i, j: (i, 0)),
                  pl.BlockSpec((tm, 2 * ROPE_DIM), lambda i, j: (i, 0)),
                  pl.BlockSpec((kl, hb * HEAD_SLOT), lambda i, j: (0, j)),
                  pl.BlockSpec((1, HEAD_SLOT), lambda i, j: (0, 0))],
        out_specs=[pl.BlockSpec((hb, tm, HEAD_SLOT), lambda i, j: (j, i, 0)),
                   pl.BlockSpec((hb, tm, V_HEAD_DIM), lambda i, j: (j, i, 0))],
        out_shape=[jax.ShapeDtypeStruct((heads, m, HEAD_SLOT), BF16),
                   jax.ShapeDtypeStruct((heads, m, V_HEAD_DIM), BF16)],
        compiler_params=_params(("parallel", "parallel"), buf),
        name="mla_kv_up",
    )(ckv, kr, tab, w_up, gk_row)


def _q_up_kernel(c_ref, tab_ref, w_ref, gq_ref, q_ref, *, hb, scale):
    c = c_ref[...]
    gq = gq_ref[...]
    tab = tab_ref[...]
    lane = lax.broadcasted_iota(jnp.int32, tab.shape, 1)
    low = lane < ROPE_DIM
    for hd in range(hb):
        qv = jnp.dot(c, w_ref[:, hd * HEAD_SLOT:(hd + 1) * HEAD_SLOT], preferred_element_type=F32)
        qn = qv[:, :NOPE_DIM]
        qr = qv[:, NOPE_DIM:]
        ssq = (jnp.sum(qn * qn, axis=1, keepdims=True)
               + jnp.sum(jnp.where(low, qr * qr, 0.0), axis=1, keepdims=True))
        r = lax.rsqrt(ssq * (1.0 / QK_HEAD_DIM) + EPS) * scale
        q_ref[hd, :, 0:NOPE_DIM] = (qn * r * gq[:, :NOPE_DIM]).astype(q_ref.dtype)
        q_ref[hd, :, NOPE_DIM:] = _fold_rope(qr * gq[:, NOPE_DIM:] * tab * r).astype(q_ref.dtype)


def _q_up(cq, tab, w_uq, gq_row, *, heads):
    m, ql = cq.shape
    hb = 2
    tm = _tile(m, 1024, 16)
    buf = 2 * (tm * ql * 2 + tm * 128 * 4 + ql * hb * HEAD_SLOT * 2 + hb * tm * HEAD_SLOT * 2) \
        + 4 * tm * HEAD_SLOT * 4
    return pl.pallas_call(
        functools.partial(_q_up_kernel, hb=hb, scale=QK_HEAD_DIM ** -0.5),
        grid=(m // tm, heads // hb),
        in_specs=[pl.BlockSpec((tm, ql), lambda i, j: (i, 0)),
                  pl.BlockSpec((tm, 2 * ROPE_DIM), lambda i, j: (i, 0)),
                  pl.BlockSpec((ql, hb * HEAD_SLOT), lambda i, j: (0, j)),
                  pl.BlockSpec((1, HEAD_SLOT), lambda i, j: (0, 0))],
        out_specs=pl.BlockSpec((hb, tm, HEAD_SLOT), lambda i, j: (j, i, 0)),
        out_shape=jax.ShapeDtypeStruct((heads, m, HEAD_SLOT), BF16),
        compiler_params=_params(("parallel", "parallel"), buf),
        name="mla_q_up",
    )(cq, tab, w_uq, gq_row)


def _attn_kernel(q_ref, k_ref, v_ref, km_ref, vm_ref, o_ref, m_sc, l_sc, acc_sc, *, hb, tq):
    qt = pl.program_id(2)
    row = lax.broadcasted_iota(jnp.int32, (tq, tq), 0)
    col = lax.broadcasted_iota(jnp.int32, (tq, tq), 1)
    diag_mask = row >= col

    def update(q, k, v, mask):
        s = lax.dot_general(q, k, NT_DIMS, preferred_element_type=F32)
        if mask is not None:
            s = jnp.where(mask, s, -jnp.inf)
        m_old = m_sc[...]
        m_new = jnp.maximum(m_old, jnp.max(s, axis=1, keepdims=True))
        alpha = jnp.exp(m_old - m_new)
        p = jnp.exp(s - m_new)
        l_sc[...] = alpha * l_sc[...] + jnp.sum(p, axis=1, keepdims=True)
        acc_sc[...] = alpha * acc_sc[...] + jnp.dot(p.astype(v.dtype), v, preferred_element_type=F32)
        m_sc[...] = m_new

    for hd in range(hb):
        q = q_ref[hd]
        s0 = lax.dot_general(q, km_ref[hd], NT_DIMS, preferred_element_type=F32)
        m0 = jnp.max(s0, axis=1, keepdims=True)
        p0 = jnp.exp(s0 - m0)
        m_sc[...] = m0
        l_sc[...] = jnp.sum(p0, axis=1, keepdims=True)
        acc_sc[...] = jnp.dot(p0.astype(BF16), vm_ref[hd], preferred_element_type=F32)

        def body(kc, carry):
            r0 = pl.multiple_of(kc * tq, tq)
            update(q, k_ref[hd, pl.ds(r0, tq), :], v_ref[hd, pl.ds(r0, tq), :], None)
            return carry

        lax.fori_loop(0, qt, body, 0)
        r0 = pl.multiple_of(qt * tq, tq)
        update(q, k_ref[hd, pl.ds(r0, tq), :], v_ref[hd, pl.ds(r0, tq), :], diag_mask)
        o_ref[:, hd * V_HEAD_DIM:(hd + 1) * V_HEAD_DIM] = (
            acc_sc[...] * (1.0 / l_sc[...])).astype(o_ref.dtype)


def _attention(q, k, v, km, vm, *, bsz, seq, n_meta):
    heads = q.shape[0]
    hb = 2
    tq = _tile(seq, 512, 128)
    nq = seq // tq
    buf = 2 * (hb * tq * HEAD_SLOT * 2 + hb * seq * (HEAD_SLOT + V_HEAD_DIM) * 2
               + hb * n_meta * (HEAD_SLOT + V_HEAD_DIM) * 2 + tq * hb * V_HEAD_DIM * 2) \
        + 3 * tq * 128 * 4 + 6 * tq * tq * 4
    return pl.pallas_call(
        functools.partial(_attn_kernel, hb=hb, tq=tq),
        grid=(bsz, heads // hb, nq),
        in_specs=[pl.BlockSpec((hb, tq, HEAD_SLOT), lambda b, h, t: (h, b * nq + t, 0)),
                  pl.BlockSpec((hb, seq, HEAD_SLOT), lambda b, h, t: (h, b, 0)),
                  pl.BlockSpec((hb, seq, V_HEAD_DIM), lambda b, h, t: (h, b, 0)),
                  pl.BlockSpec((hb, n_meta, HEAD_SLOT), lambda b, h, t: (h, b, 0)),
                  pl.BlockSpec((hb, n_meta, V_HEAD_DIM), lambda b, h, t: (h, b, 0))],
        out_specs=pl.BlockSpec((tq, hb * V_HEAD_DIM), lambda b, h, t: (b * nq + t, h)),
        out_shape=jax.ShapeDtypeStruct((bsz * seq, heads * V_HEAD_DIM), BF16),
        scratch_shapes=[pltpu.VMEM((tq, 1), F32), pltpu.VMEM((tq, 1), F32),
                        pltpu.VMEM((tq, V_HEAD_DIM), F32)],
        compiler_params=_params(("parallel", "parallel", "arbitrary"), buf),
        name="mla_attention",
    )(q, k, v, km, vm)


def _rope_table(pos):
    half = ROPE_DIM // 2
    inv = 1.0 / (ROPE_THETA ** (jnp.arange(0, ROPE_DIM, 2, dtype=F32) / ROPE_DIM))
    ang = pos.astype(F32)[:, None] * inv
    c, s = jnp.cos(ang), jnp.sin(ang)
    del half
    return jnp.concatenate([c, c, -s, s], axis=-1)


def _with_swapped_rope(w, lead):
    half = ROPE_DIM // 2
    return jnp.concatenate([w[..., :lead + ROPE_DIM], w[..., lead + half:lead + ROPE_DIM],
                            w[..., lead:lead + half]], axis=-1)


def kernel(x, positions, meta_tokens, a_norm_g, a_w_in, a_b_i, a_b_f, a_out_norm_g, a_w_out,
           ffn_norm_g, ffn_w_gate_up, ffn_w_down, kv_norm_g, kv_w_down, kv_latent_norm_g,
           kv_w_up, k_norm_g, b_norm_g, b_w_dq, b_q_latent_norm_g, b_w_uq, q_norm_g, b_w_o):
    bsz, seq, d = x.shape
    n_meta = meta_tokens.shape[0]
    n_a = a_w_in.shape[0]
    depth = ffn_w_gate_up.shape[0]
    nh = MLSTM_HEADS
    v_tot = a_w_out.shape[1]
    qk_tot = (a_w_in.shape[2] - 2 * v_tot - 2 * nh) // 2
    dk, dv = qk_tot // nh, v_tot // nh
    kv_lora = kv_latent_norm_g.shape[0]
    heads = kv_w_up.shape[1] // (NOPE_DIM + V_HEAD_DIM)
    row2 = lambda g: g.reshape(1, -1).astype(F32)

    streams = [x.reshape(bsz * seq, d),
               jnp.broadcast_to(meta_tokens[None].astype(x.dtype), (bsz, n_meta, d)).reshape(bsz * n_meta, d)]
    chunks = [_tile(seq, MLSTM_CHUNK, 128), n_meta]

    def swiglu(hs, layer):
        w_gu = ffn_w_gate_up[layer].astype(BF16)
        w_dn = ffn_w_down[layer].astype(BF16)
        g = row2(ffn_norm_g[layer])
        return [_mm_resid(_ffn_up(h, g, w_gu), w_dn, h, tm_pref=512, tn_pref=512, name="ffn_down")
                for h in hs]

    layer = 0
    for i in range(n_a):
        w_in = a_w_in[i]
        w_main = w_in[:, :2 * qk_tot + 2 * v_tot].astype(BF16)
        wg_t = w_in[:, 2 * qk_tot + 2 * v_tot:].T.astype(BF16)
        b_col = jnp.concatenate([a_b_i[i], a_b_f[i]]).reshape(2 * nh, 1).astype(F32)
        w_out = a_w_out[i].astype(BF16)
        g_in = row2(a_norm_g[i])
        g_out = row2(a_out_norm_g[i])
        state = (jnp.zeros((bsz, nh, dk, dv), F32), jnp.zeros((bsz, nh, 1, dk), F32),
                 jnp.zeros((bsz, nh, 8, 128), F32))
        new = [None, None]
        for s_idx in (1, 0):
            h = streams[s_idx]
            proj, gates_t = _in_proj(h, g_in, w_main, wg_t, b_col)
            mixed, state = _mlstm(proj, gates_t, g_out, state, bsz=bsz, chunk=chunks[s_idx])
            new[s_idx] = _mm_resid(mixed, w_out, h, tm_pref=1024, tn_pref=512, name="mlstm_out_proj")
        streams = swiglu(new, layer)
        layer += 1

    pos_real = (positions.astype(jnp.int32) + n_meta).reshape(bsz * seq)
    pos_meta = jnp.broadcast_to(jnp.arange(n_meta, dtype=jnp.int32)[None], (bsz, n_meta)).reshape(-1)
    tabs = [_rope_table(pos_real), _rope_table(pos_meta)]
    w_kv = _with_swapped_rope(kv_w_down, kv_lora).astype(BF16)
    w_up = kv_w_up.astype(BF16)
    gk_row = row2(_with_swapped_rope(k_norm_g, NOPE_DIM))
    g_kv = row2(kv_norm_g)
    g_lat = row2(kv_latent_norm_g)

    h_real, h_meta = streams
    ckv_m, kr_m = _kvq_down(h_meta, g_kv, w_kv, g_lat)
    k_meta, v_meta = _kv_up(ckv_m, kr_m, tabs[1], w_up, gk_row, heads=heads)
    k_real = v_real = None
    for j in range(depth - n_a):
        g_b = row2(b_norm_g[j])
        w_dq = b_w_dq[j].astype(BF16)
        g_ql = row2(b_q_latent_norm_g[j])
        ql = w_dq.shape[1]
        w_uq = _with_swapped_rope(b_w_uq[j].reshape(ql, heads, QK_HEAD_DIM), NOPE_DIM)
        w_uq = w_uq.reshape(ql, heads * HEAD_SLOT).astype(BF16)
        gq_row = row2(_with_swapped_rope(q_norm_g[j], NOPE_DIM))
        w_o = b_w_o[j].astype(BF16)
        if j == 0:
            ckv, kr, cq = _kvq_down(h_real, g_kv, w_kv, g_lat, (g_b, w_dq, g_ql))
            k_real, v_real = _kv_up(ckv, kr, tabs[0], w_up, gk_row, heads=heads)
        else:
            _, _, cq = _kvq_down(h_real, g_kv, w_kv, g_lat, (g_b, w_dq, g_ql))
        q = _q_up(cq, tabs[0], w_uq, gq_row, heads=heads)
        o = _attention(q, k_real, v_real, k_meta, v_meta, bsz=bsz, seq=seq, n_meta=n_meta)
        h_real = _mm_resid(o, w_o, h_real, tm_pref=512, tn_pref=512, name="mla_out_proj")
        (h_real,) = swiglu([h_real], layer)
        layer += 1
    return h_real.reshape(bsz, seq, d)
```

```python
import functools

import jax
import jax.numpy as jnp
from jax import lax
from jax.experimental import pallas as pl
from jax.experimental.pallas import tpu as pltpu

F32 = jnp.float32
BF16 = jnp.bfloat16

EPS = 1e-6
MLSTM_HEADS = 8
GATE_CAP = 15.0
NOPE_DIM = 128
ROPE_DIM = 64
V_HEAD_DIM = 128
QK_HEAD_DIM = NOPE_DIM + ROPE_DIM
HEAD_SLOT = 2 * NOPE_DIM
ROPE_THETA = 10000.0
LOG2_E = 1.4426950408889634
MLSTM_CHUNK = 256
FFN_TILE = 512
FFN_SUBTILE = 256
NORM_ROWS = 16

V7X_VMEM_BYTES = 64 * 1024 * 1024
VMEM_MARGIN_BYTES = 8 * 1024 * 1024
VMEM_CAP_BYTES = 58 * 1024 * 1024

NT_DIMS = (((1,), (1,)), ((), ()))
TN_DIMS = (((0,), (0,)), ((), ()))


def _params(semantics, buffer_bytes):
    limit = min(int(buffer_bytes) + VMEM_MARGIN_BYTES, VMEM_CAP_BYTES)
    return pltpu.CompilerParams(dimension_semantics=semantics, vmem_limit_bytes=limit)


def _tile(n, pref, align):
    if n <= pref:
        return n
    t = (pref // align) * align
    while t >= align:
        if n % t == 0:
            return t
        t -= align
    return n


def _rms_to_bf16(h_ref, gains, outs):
    tm = h_ref.shape[0]
    rc = NORM_ROWS if tm % NORM_ROWS == 0 else tm

    def body(c, carry):
        r0 = pl.multiple_of(c * rc, rc)
        x = h_ref[pl.ds(r0, rc), :]
        y = x * lax.rsqrt(jnp.mean(x * x, axis=-1, keepdims=True) + EPS)
        for g, o_ref in zip(gains, outs):
            o_ref[pl.ds(r0, rc), :] = (y * g).astype(BF16)
        return carry

    lax.fori_loop(0, tm // rc, body, 0)


def _sigmoid(x):
    return 1.0 / (1.0 + jnp.exp(-x))


def _in_proj_kernel(h_ref, g_ref, w_ref, wgt_ref, bcol_ref, o_ref, gt_ref, xn_ref):
    @pl.when(pl.program_id(1) == 0)
    def _():
        _rms_to_bf16(h_ref, [g_ref[...]], [xn_ref])
        z = lax.dot_general(wgt_ref[...], xn_ref[...], NT_DIMS, preferred_element_type=F32)
        z = z + bcol_ref[...]
        capped = GATE_CAP * jnp.tanh(z / GATE_CAP)
        logf = jnp.minimum(capped, 0.0) - jnp.log(1.0 + jnp.exp(-jnp.abs(capped)))
        row = lax.broadcasted_iota(jnp.int32, z.shape, 0)
        gt_ref[...] = jnp.where(row < MLSTM_HEADS, capped, logf)

    o_ref[...] = jnp.dot(xn_ref[...], w_ref[...], preferred_element_type=F32).astype(o_ref.dtype)


def _in_proj(h, g, w_main, wg_t, b_col):
    m, d = h.shape
    n = w_main.shape[1]
    tm = _tile(m, 1024, 16)
    tn = _tile(n, 1024, 128)
    ng = wg_t.shape[0]
    single = tm * d * 4 > 8 * 1024 * 1024
    h_spec = pl.BlockSpec((tm, d), lambda i, j: (i, 0),
                          pipeline_mode=pl.Buffered(1) if single else None)
    buf = (tm * d * 4 * (1 if single else 2) + tm * d * 2 + 2 * d * tn * 2 + 2 * tm * tn * 2
           + 2 * ng * d * 2 + 2 * ng * tm * 4 + tm * tn * 4)
    return pl.pallas_call(
        _in_proj_kernel,
        grid=(m // tm, n // tn),
        in_specs=[h_spec,
                  pl.BlockSpec((1, d), lambda i, j: (0, 0)),
                  pl.BlockSpec((d, tn), lambda i, j: (0, j)),
                  pl.BlockSpec((ng, d), lambda i, j: (0, 0)),
                  pl.BlockSpec((ng, 1), lambda i, j: (0, 0))],
        out_specs=[pl.BlockSpec((tm, tn), lambda i, j: (i, j)),
                   pl.BlockSpec((ng, tm), lambda i, j: (0, i))],
        out_shape=[jax.ShapeDtypeStruct((m, n), BF16),
                   jax.ShapeDtypeStruct((ng, m), F32)],
        scratch_shapes=[pltpu.VMEM((tm, d), BF16)],
        compiler_params=_params(("parallel", "arbitrary"), buf),
        name="mlstm_in_proj",
    )(h, g, w_main, wg_t, b_col)


def _mlstm_kernel(q_ref, k_ref, v_ref, og_ref, gc_ref, gr_ref, gout_ref, c0_ref, n0_ref, m0_ref,
                  o_ref, ct_ref, nt_ref, mt_ref, c_sc, n_sc, m_sc, *, k_scale):
    hd = pl.program_id(1)
    ci = pl.program_id(2)
    t = q_ref.shape[0]

    @pl.when(ci == 0)
    def _():
        c_sc[...] = c0_ref[0, 0]
        n_sc[...] = n0_ref[0, 0]
        m_sc[...] = m0_ref[0, 0]

    q = q_ref[...]
    k = k_ref[...] * k_scale
    v = v_ref[...]
    gc = gc_ref[0]
    gr = gr_ref[0]
    lane = lax.broadcasted_iota(jnp.int32, gc.shape, 1)
    sub = lax.broadcasted_iota(jnp.int32, gr.shape, 0)
    logi_c = jnp.sum(jnp.where(lane == hd, gc, 0.0), axis=1, keepdims=True)
    logf_c = jnp.sum(jnp.where(lane == hd + MLSTM_HEADS, gc, 0.0), axis=1, keepdims=True)
    logi_r = jnp.sum(jnp.where(sub == hd, gr, 0.0), axis=0, keepdims=True)
    logf_r = jnp.sum(jnp.where(sub == hd + MLSTM_HEADS, gr, 0.0), axis=0, keepdims=True)

    row = lax.broadcasted_iota(jnp.int32, (t, t), 0)
    col = lax.broadcasted_iota(jnp.int32, (t, t), 1)
    causal = row >= col
    b_c = jnp.sum(jnp.where(causal, logf_r, 0.0), axis=1, keepdims=True)
    b_r = jnp.sum(jnp.where(causal, 0.0, logf_c), axis=0, keepdims=True) + logf_r
    m_prev = m_sc[0:1, 0:1]
    c_prev = c_sc[...]
    n_prev = n_sc[...]

    d = jnp.where(causal, b_c - b_r + logi_r, -jnp.inf)
    inter = b_c + m_prev
    m_t = jnp.maximum(inter, jnp.max(d, axis=1, keepdims=True))
    w = jnp.exp(d - m_t)
    a = jnp.exp(inter - m_t)
    s = lax.dot_general(q, k, NT_DIMS, preferred_element_type=F32) * w
    num = (a * jnp.dot(q, c_prev.astype(BF16), preferred_element_type=F32)
           + jnp.dot(s.astype(BF16), v, preferred_element_type=F32))
    den = (a * jnp.sum(q.astype(F32) * n_prev, axis=1, keepdims=True)
           + jnp.sum(s, axis=1, keepdims=True))
    hval = num * (1.0 / jnp.maximum(jnp.abs(den), jnp.exp(-m_t)))

    ms = jnp.mean(hval * hval, axis=1, keepdims=True)
    hn = hval * lax.rsqrt(ms + EPS) * gout_ref[...]
    o_ref[...] = (hn * _sigmoid(og_ref[...].astype(F32))).astype(o_ref.dtype)

    b_last = jnp.sum(logf_r, axis=1, keepdims=True)
    g_c = b_last - b_c + logi_c
    m_new = jnp.maximum(b_last + m_prev, jnp.max(g_c, axis=0, keepdims=True))
    decay = jnp.exp(b_last + m_prev - m_new)
    kw = k.astype(F32) * jnp.exp(g_c - m_new)
    c_new = decay * c_prev + lax.dot_general(kw.astype(BF16), v, TN_DIMS, preferred_element_type=F32)
    n_new = decay * n_prev + jnp.sum(kw, axis=0, keepdims=True)
    c_sc[...] = c_new
    n_sc[...] = n_new
    m_sc[...] = jnp.broadcast_to(m_new, m_sc.shape)

    @pl.when(ci == pl.num_programs(2) - 1)
    def _():
        ct_ref[0, 0] = c_new
        nt_ref[0, 0] = n_new
        mt_ref[0, 0] = jnp.broadcast_to(m_new, m_sc.shape)


def _mlstm(proj, gates_t, g_out, state, *, bsz, chunk):
    m = proj.shape[0]
    c0, n0, m0 = state
    dk, dv = c0.shape[2], c0.shape[3]
    nh = MLSTM_HEADS
    nc = m // (bsz * chunk)
    g_row = gates_t.reshape(2 * nh, bsz * nc, chunk).transpose(1, 0, 2)
    g_col = g_row.transpose(0, 2, 1)
    kq = nh
    kv = (2 * nh * dk) // dv
    ko = kv + nh
    rowblk = lambda b, h, c: b * nc + c
    st_spec = lambda shp: pl.BlockSpec((1, 1) + shp, lambda b, h, c: (b, h, 0, 0))
    buf = 2 * (2 * chunk * dk * 2 + 2 * chunk * dv * 2 + chunk * dv * 2) + 8 * dk * dv * 4 + (1 << 20)
    out, ct, nt, mt = pl.pallas_call(
        functools.partial(_mlstm_kernel, k_scale=dk ** -0.5),
        grid=(bsz, nh, nc),
        in_specs=[pl.BlockSpec((chunk, dk), lambda b, h, c: (rowblk(b, h, c), h)),
                  pl.BlockSpec((chunk, dk), lambda b, h, c: (rowblk(b, h, c), kq + h)),
                  pl.BlockSpec((chunk, dv), lambda b, h, c: (rowblk(b, h, c), kv + h)),
                  pl.BlockSpec((chunk, dv), lambda b, h, c: (rowblk(b, h, c), ko + h)),
                  pl.BlockSpec((1, chunk, 2 * nh), lambda b, h, c: (rowblk(b, h, c), 0, 0)),
                  pl.BlockSpec((1, 2 * nh, chunk), lambda b, h, c: (rowblk(b, h, c), 0, 0)),
                  pl.BlockSpec((1, dv), lambda b, h, c: (0, h)),
                  st_spec((dk, dv)), st_spec((1, dk)), st_spec((8, 128))],
        out_specs=[pl.BlockSpec((chunk, dv), lambda b, h, c: (rowblk(b, h, c), h)),
                   st_spec((dk, dv)), st_spec((1, dk)), st_spec((8, 128))],
        out_shape=[jax.ShapeDtypeStruct((m, nh * dv), BF16),
                   jax.ShapeDtypeStruct(c0.shape, F32),
                   jax.ShapeDtypeStruct(n0.shape, F32),
                   jax.ShapeDtypeStruct(m0.shape, F32)],
        scratch_shapes=[pltpu.VMEM((dk, dv), F32), pltpu.VMEM((1, dk), F32), pltpu.VMEM((8, 128), F32)],
        compiler_params=_params(("parallel", "parallel", "arbitrary"), buf),
        name="mlstm_chunks",
    )(proj, proj, proj, proj, g_col, g_row, g_out, c0, n0, m0)
    return out, (ct, nt, mt)


def _mm_resid_kernel(x_ref, w_ref, r_ref, o_ref):
    o_ref[...] = r_ref[...] + jnp.dot(x_ref[...], w_ref[...], preferred_element_type=F32)


def _mm_resid(x, w, resid, *, tm_pref, tn_pref, name):
    m = x.shape[0]
    k, n = w.shape
    tm = _tile(m, tm_pref, 16)
    tn = _tile(n, tn_pref, 128)
    buf = 2 * (tm * k * 2 + k * tn * 2 + 2 * tm * tn * 4) + tm * tn * 4
    return pl.pallas_call(
        _mm_resid_kernel,
        grid=(m // tm, n // tn),
        in_specs=[pl.BlockSpec((tm, k), lambda i, j: (i, 0)),
                  pl.BlockSpec((k, tn), lambda i, j: (0, j)),
                  pl.BlockSpec((tm, tn), lambda i, j: (i, j))],
        out_specs=pl.BlockSpec((tm, tn), lambda i, j: (i, j)),
        out_shape=jax.ShapeDtypeStruct((m, n), F32),
        compiler_params=_params(("parallel", "parallel"), buf),
        name=name,
    )(x, w, resid)


def _ffn_up_kernel(h_ref, g_ref, wg_ref, wu_ref, o_ref, xn_ref):
    @pl.when(pl.program_id(1) == 0)
    def _():
        _rms_to_bf16(h_ref, [g_ref[...]], [xn_ref])

    x = xn_ref[...]
    sub = FFN_SUBTILE if o_ref.shape[1] % FFN_SUBTILE == 0 else o_ref.shape[1]
    for c in range(o_ref.shape[1] // sub):
        cs = slice(c * sub, (c + 1) * sub)
        gate = jnp.dot(x, wg_ref[:, cs], preferred_element_type=F32)
        up = jnp.dot(x, wu_ref[:, cs], preferred_element_type=F32)
        o_ref[:, cs] = (gate * _sigmoid(gate) * up).astype(o_ref.dtype)


def _ffn_up(h, g, w_gate_up):
    m, d = h.shape
    f = w_gate_up.shape[1] // 2
    tm = _tile(m, 1024, 16)
    tn = _tile(f, FFN_TILE, 128)
    nj = f // tn
    single = tm * d * 4 > 8 * 1024 * 1024
    h_spec = pl.BlockSpec((tm, d), lambda i, j: (i, 0),
                          pipeline_mode=pl.Buffered(1) if single else None)
    buf = (tm * d * 4 * (1 if single else 2) + tm * d * 2 + 4 * d * tn * 2 + 2 * tm * tn * 2
           + 2 * tm * tn * 4)
    return pl.pallas_call(
        _ffn_up_kernel,
        grid=(m // tm, nj),
        in_specs=[h_spec,
                  pl.BlockSpec((1, d), lambda i, j: (0, 0)),
                  pl.BlockSpec((d, tn), lambda i, j: (0, j)),
                  pl.BlockSpec((d, tn), lambda i, j: (0, j + nj))],
        out_specs=pl.BlockSpec((tm, tn), lambda i, j: (i, j)),
        out_shape=jax.ShapeDtypeStruct((m, f), BF16),
        scratch_shapes=[pltpu.VMEM((tm, d), BF16)],
        compiler_params=_params(("parallel", "arbitrary"), buf),
        name="ffn_up",
    )(h, g, w_gate_up, w_gate_up)


def _kvq_down_kernel(*refs, with_q, kv_lora):
    if with_q:
        (h_ref, gkv_ref, gb_ref, wkv_ref, wdq_ref, glat_ref, gql_ref,
         ckv_ref, kr_ref, cq_ref, xa_ref, xb_ref) = refs
        _rms_to_bf16(h_ref, [gkv_ref[...], gb_ref[...]], [xa_ref, xb_ref])
    else:
        h_ref, gkv_ref, wkv_ref, glat_ref, ckv_ref, kr_ref, xa_ref = refs
        _rms_to_bf16(h_ref, [gkv_ref[...]], [xa_ref])

    a = jnp.dot(xa_ref[...], wkv_ref[...], preferred_element_type=F32)
    ckv = a[:, :kv_lora]
    ckv_ref[...] = (ckv * lax.rsqrt(jnp.mean(ckv * ckv, axis=-1, keepdims=True) + EPS)
                    * glat_ref[...]).astype(ckv_ref.dtype)
    kr_ref[...] = a[:, kv_lora:]
    if with_q:
        cq = jnp.dot(xb_ref[...], wdq_ref[...], preferred_element_type=F32)
        cq_ref[...] = (cq * lax.rsqrt(jnp.mean(cq * cq, axis=-1, keepdims=True) + EPS)
                       * gql_ref[...]).astype(cq_ref.dtype)


def _kvq_down(h, g_kv, w_kv, g_lat, q_parts=None):
    m, d = h.shape
    kv_lora = g_lat.shape[1]
    nkv = w_kv.shape[1]
    tm = _tile(m, 256, 16)
    row = lambda shp: pl.BlockSpec(shp, lambda i: (i, 0))
    full = lambda shp: pl.BlockSpec(shp, lambda i: (0, 0))
    with_q = q_parts is not None
    if with_q:
        g_b, w_dq, g_ql = q_parts
        nq = w_dq.shape[1]
        args = (h, g_kv, g_b, w_kv, w_dq, g_lat, g_ql)
        in_specs = [row((tm, d)), full((1, d)), full((1, d)), full((d, nkv)), full((d, nq)),
                    full((1, kv_lora)), full((1, nq))]
        out_specs = [row((tm, kv_lora)), row((tm, nkv - kv_lora)), row((tm, nq))]
        out_shape = [jax.ShapeDtypeStruct((m, kv_lora), BF16),
                     jax.ShapeDtypeStruct((m, nkv - kv_lora), F32),
                     jax.ShapeDtypeStruct((m, nq), BF16)]
        scratch = [pltpu.VMEM((tm, d), BF16), pltpu.VMEM((tm, d), BF16)]
        buf = 2 * tm * d * 4 + 2 * tm * d * 2 + 2 * d * (nkv + nq) * 2 + 2 * tm * (nkv + nq) * 4
    else:
        args = (h, g_kv, w_kv, g_lat)
        in_specs = [row((tm, d)), full((1, d)), full((d, nkv)), full((1, kv_lora))]
        out_specs = [row((tm, kv_lora)), row((tm, nkv - kv_lora))]
        out_shape = [jax.ShapeDtypeStruct((m, kv_lora), BF16),
                     jax.ShapeDtypeStruct((m, nkv - kv_lora), F32)]
        scratch = [pltpu.VMEM((tm, d), BF16)]
        buf = 2 * tm * d * 4 + tm * d * 2 + 2 * d * nkv * 2 + 2 * tm * nkv * 4
    return pl.pallas_call(
        functools.partial(_kvq_down_kernel, with_q=with_q, kv_lora=kv_lora),
        grid=(m // tm,),
        in_specs=in_specs, out_specs=out_specs, out_shape=out_shape, scratch_shapes=scratch,
        compiler_params=_params(("parallel",), buf),
        name="mla_down_q" if with_q else "mla_down_kv",
    )(*args)


def _fold_rope(y):
    return y + pltpu.roll(y, ROPE_DIM, 1)


def _kv_up_kernel(c_ref, kr_ref, tab_ref, w_ref, gk_ref, k_ref, v_ref, y_sc, *, hb, rows):
    y_sc[...] = jnp.dot(c_ref[...], w_ref[...], preferred_element_type=F32)
    gk = gk_ref[...]
    low = lax.broadcasted_iota(jnp.int32, (rows, 2 * ROPE_DIM), 1) < ROPE_DIM

    def body(i, carry):
        rs = pl.ds(pl.multiple_of(i * rows, rows), rows)
        kr = kr_ref[rs, :]
        ssq_rope = jnp.sum(jnp.where(low, kr * kr, 0.0), axis=1, keepdims=True)
        pt = kr * gk[:, NOPE_DIM:] * tab_ref[rs, :]
        for hd in range(hb):
            kn = y_sc[rs, hd * HEAD_SLOT:hd * HEAD_SLOT + NOPE_DIM]
            r = lax.rsqrt((jnp.sum(kn * kn, axis=1, keepdims=True) + ssq_rope) * (1.0 / QK_HEAD_DIM) + EPS)
            k_ref[hd, rs, 0:NOPE_DIM] = (kn * r * gk[:, :NOPE_DIM]).astype(k_ref.dtype)
            k_ref[hd, rs, NOPE_DIM:] = jnp.where(low, _fold_rope(pt * r), 0.0).astype(k_ref.dtype)
            v_ref[hd, rs, :] = y_sc[rs, hd * HEAD_SLOT + NOPE_DIM:(hd + 1) * HEAD_SLOT].astype(v_ref.dtype)
        return carry

    lax.fori_loop(0, c_ref.shape[0] // rows, body, 0)


def _kv_up(ckv, kr, tab, w_up, gk_row, *, heads):
    m, kl = ckv.shape
    hb = 4
    tm = _tile(m, 1024, 16)
    rows = _tile(tm, 64, 16)
    buf = 2 * (tm * kl * 2 + 2 * tm * 128 * 4 + kl * hb * HEAD_SLOT * 2
               + hb * tm * HEAD_SLOT * 2 + hb * tm * V_HEAD_DIM * 2) + 2 * tm * hb * HEAD_SLOT * 4
    return pl.pallas_call(
        functools.partial(_kv_up_kernel, hb=hb, rows=rows),
        grid=(m // tm, heads // hb),
        in_specs=[pl.BlockSpec((tm, kl), lambda i, j: (i, 0)),
                  pl.BlockSpec((tm, 2 * ROPE_DIM), lambda i, j: (i, 0)),
                  pl.BlockSpec((tm, 2 * ROPE_DIM), lambda i, j: (i, 0)),
                  pl.BlockSpec((kl, hb * HEAD_SLOT), lambda i, j: (0, j)),
                  pl.BlockSpec((1, HEAD_SLOT), lambda i, j: (0, 0))],
        out_specs=[pl.BlockSpec((hb, tm, HEAD_SLOT), lambda i, j: (j, i, 0)),
                   pl.BlockSpec((hb, tm, V_HEAD_DIM), lambda i, j: (j, i, 0))],
        out_shape=[jax.ShapeDtypeStruct((heads, m, HEAD_SLOT), BF16),
                   jax.ShapeDtypeStruct((heads, m, V_HEAD_DIM), BF16)],
        scratch_shapes=[pltpu.VMEM((tm, hb * HEAD_SLOT), F32)],
        compiler_params=_params(("parallel", "parallel"), buf),
        name="mla_kv_up",
    )(ckv, kr, tab, w_up, gk_row)


def _q_up_kernel(c_ref, tab_ref, w_ref, gq_ref, q_ref, y_sc, *, hb, rows, scale):
    y_sc[...] = jnp.dot(c_ref[...], w_ref[...], preferred_element_type=F32)
    gq = gq_ref[...]
    low = lax.broadcasted_iota(jnp.int32, (rows, 2 * ROPE_DIM), 1) < ROPE_DIM

    def body(i, carry):
        rs = pl.ds(pl.multiple_of(i * rows, rows), rows)
        gtab = gq[:, NOPE_DIM:] * tab_ref[rs, :]
        for hd in range(hb):
            qn = y_sc[rs, hd * HEAD_SLOT:hd * HEAD_SLOT + NOPE_DIM]
            qr = y_sc[rs, hd * HEAD_SLOT + NOPE_DIM:(hd + 1) * HEAD_SLOT]
            ssq = (jnp.sum(qn * qn, axis=1, keepdims=True)
                   + jnp.sum(jnp.where(low, qr * qr, 0.0), axis=1, keepdims=True))
            r = lax.rsqrt(ssq * (1.0 / QK_HEAD_DIM) + EPS) * scale
            q_ref[hd, rs, 0:NOPE_DIM] = (qn * r * gq[:, :NOPE_DIM]).astype(q_ref.dtype)
            q_ref[hd, rs, NOPE_DIM:] = _fold_rope(qr * gtab * r).astype(q_ref.dtype)
        return carry

    lax.fori_loop(0, c_ref.shape[0] // rows, body, 0)


def _q_up(cq, tab, w_uq, gq_row, *, heads):
    m, ql = cq.shape
    hb = 4
    tm = _tile(m, 1024, 16)
    rows = _tile(tm, 64, 16)
    buf = 2 * (tm * ql * 2 + tm * 128 * 4 + ql * hb * HEAD_SLOT * 2 + hb * tm * HEAD_SLOT * 2) \
        + 2 * tm * hb * HEAD_SLOT * 4
    return pl.pallas_call(
        functools.partial(_q_up_kernel, hb=hb, rows=rows, scale=QK_HEAD_DIM ** -0.5 * LOG2_E),
        grid=(m // tm, heads // hb),
        in_specs=[pl.BlockSpec((tm, ql), lambda i, j: (i, 0)),
                  pl.BlockSpec((tm, 2 * ROPE_DIM), lambda i, j: (i, 0)),
                  pl.BlockSpec((ql, hb * HEAD_SLOT), lambda i, j: (0, j)),
                  pl.BlockSpec((1, HEAD_SLOT), lambda i, j: (0, 0))],
        out_specs=pl.BlockSpec((hb, tm, HEAD_SLOT), lambda i, j: (j, i, 0)),
        out_shape=jax.ShapeDtypeStruct((heads, m, HEAD_SLOT), BF16),
        scratch_shapes=[pltpu.VMEM((tm, hb * HEAD_SLOT), F32)],
        compiler_params=_params(("parallel", "parallel"), buf),
        name="mla_q_up",
    )(cq, tab, w_uq, gq_row)


def _attn_kernel(q_ref, k_ref, v_ref, km_ref, vm_ref, o_ref, va_sc, vma_sc, m_sc, acc_sc, *, hb, tq):
    qt = pl.program_id(2)
    lanes = V_HEAD_DIM

    @pl.when(qt == 0)
    def _():
        for hd in range(hb):
            va_sc[hd, :, 0:lanes] = v_ref[hd]
            va_sc[hd, :, lanes:] = jnp.ones((va_sc.shape[1], lanes), va_sc.dtype)
            vma_sc[hd, :, 0:lanes] = vm_ref[hd]
            vma_sc[hd, :, lanes:] = jnp.ones((vma_sc.shape[1], lanes), vma_sc.dtype)

    def update(hd, r0, masked):
        k = k_ref[hd, pl.ds(r0, tq), :]
        s = lax.dot_general(q_ref[hd], k, NT_DIMS, preferred_element_type=F32)
        if masked:
            row = lax.broadcasted_iota(jnp.int32, (tq, tq), 0)
            col = lax.broadcasted_iota(jnp.int32, (tq, tq), 1)
            s = jnp.where(row >= col, s, -jnp.inf)
        m_old = m_sc[hd]
        m_new = jnp.maximum(m_old, jnp.max(s, axis=1, keepdims=True))
        alpha = jnp.exp2(m_old - m_new)
        p = jnp.exp2(s - jnp.concatenate([m_new] * (tq // lanes), axis=1))
        pv = jnp.dot(p.astype(BF16), va_sc[hd, pl.ds(r0, tq), :], preferred_element_type=F32)
        acc_sc[hd] = jnp.concatenate([alpha, alpha], axis=1) * acc_sc[hd] + pv
        m_sc[hd] = m_new

    for hd in range(hb):
        s0 = lax.dot_general(q_ref[hd], km_ref[hd], NT_DIMS, preferred_element_type=F32)
        m0 = jnp.max(s0, axis=1, keepdims=True)
        p0 = jnp.exp2(s0 - m0)
        m_sc[hd] = jnp.broadcast_to(m0, (tq, lanes))
        acc_sc[hd] = jnp.dot(p0.astype(BF16), vma_sc[hd], preferred_element_type=F32)

    def body(kc, carry):
        for half in range(2):
            r0 = pl.multiple_of((2 * kc + half) * tq, tq)
            for hd in range(hb):
                update(hd, r0, False)
        return carry

    lax.fori_loop(0, qt // 2, body, 0)

    @pl.when(qt % 2 == 1)
    def _():
        r0 = pl.multiple_of((qt - 1) * tq, tq)
        for hd in range(hb):
            update(hd, r0, False)

    r0 = pl.multiple_of(qt * tq, tq)
    for hd in range(hb):
        update(hd, r0, True)
    for hd in range(hb):
        acc = acc_sc[hd]
        o_ref[:, hd * lanes:(hd + 1) * lanes] = (acc[:, :lanes] * (1.0 / acc[:, lanes:])).astype(o_ref.dtype)


def _attention(q, k, v, km, vm, *, bsz, seq, n_meta):
    heads = q.shape[0]
    hb = 4
    tq = _tile(seq, 512, 128)
    nq = seq // tq
    buf = 2 * (hb * tq * HEAD_SLOT * 2 + hb * seq * (HEAD_SLOT + V_HEAD_DIM) * 2
               + hb * n_meta * (HEAD_SLOT + V_HEAD_DIM) * 2 + tq * hb * V_HEAD_DIM * 2) \
        + hb * (seq + n_meta) * 2 * V_HEAD_DIM * 2 + hb * tq * 3 * 128 * 4 + 6 * tq * tq * 4
    return pl.pallas_call(
        functools.partial(_attn_kernel, hb=hb, tq=tq),
        grid=(bsz, heads // hb, nq),
        in_specs=[pl.BlockSpec((hb, tq, HEAD_SLOT), lambda b, h, t: (h, b * nq + t, 0)),
                  pl.BlockSpec((hb, seq, HEAD_SLOT), lambda b, h, t: (h, b, 0)),
                  pl.BlockSpec((hb, seq, V_HEAD_DIM), lambda b, h, t: (h, b, 0)),
                  pl.BlockSpec((hb, n_meta, HEAD_SLOT), lambda b, h, t: (h, b, 0)),
                  pl.BlockSpec((hb, n_meta, V_HEAD_DIM), lambda b, h, t: (h, b, 0))],
        out_specs=pl.BlockSpec((tq, hb * V_HEAD_DIM), lambda b, h, t: (b * nq + t, h)),
        out_shape=jax.ShapeDtypeStruct((bsz * seq, heads * V_HEAD_DIM), BF16),
        scratch_shapes=[pltpu.VMEM((hb, seq, 2 * V_HEAD_DIM), BF16),
                        pltpu.VMEM((hb, n_meta, 2 * V_HEAD_DIM), BF16),
                        pltpu.VMEM((hb, tq, V_HEAD_DIM), F32),
                        pltpu.VMEM((hb, tq, 2 * V_HEAD_DIM), F32)],
        compiler_params=_params(("parallel", "parallel", "arbitrary"), buf),
        name="mla_attention",
    )(q, k, v, km, vm)


def _rope_table(pos):
    half = ROPE_DIM // 2
    inv = 1.0 / (ROPE_THETA ** (jnp.arange(0, ROPE_DIM, 2, dtype=F32) / ROPE_DIM))
    ang = pos.astype(F32)[:, None] * inv
    c, s = jnp.cos(ang), jnp.sin(ang)
    del half
    return jnp.concatenate([c, c, -s, s], axis=-1)


def _with_swapped_rope(w, lead):
    half = ROPE_DIM // 2
    return jnp.concatenate([w[..., :lead + ROPE_DIM], w[..., lead + half:lead + ROPE_DIM],
                            w[..., lead:lead + half]], axis=-1)


def kernel(x, positions, meta_tokens, a_norm_g, a_w_in, a_b_i, a_b_f, a_out_norm_g, a_w_out,
           ffn_norm_g, ffn_w_gate_up, ffn_w_down, kv_norm_g, kv_w_down, kv_latent_norm_g,
           kv_w_up, k_norm_g, b_norm_g, b_w_dq, b_q_latent_norm_g, b_w_uq, q_norm_g, b_w_o):
    bsz, seq, d = x.shape
    n_meta = meta_tokens.shape[0]
    n_a = a_w_in.shape[0]
    depth = ffn_w_gate_up.shape[0]
    nh = MLSTM_HEADS
    v_tot = a_w_out.shape[1]
    qk_tot = (a_w_in.shape[2] - 2 * v_tot - 2 * nh) // 2
    dk, dv = qk_tot // nh, v_tot // nh
    kv_lora = kv_latent_norm_g.shape[0]
    heads = kv_w_up.shape[1] // (NOPE_DIM + V_HEAD_DIM)
    row2 = lambda g: g.reshape(1, -1).astype(F32)

    streams = [x.reshape(bsz * seq, d),
               jnp.broadcast_to(meta_tokens[None].astype(x.dtype), (bsz, n_meta, d)).reshape(bsz * n_meta, d)]
    chunks = [_tile(seq, MLSTM_CHUNK, 128), n_meta]

    def swiglu(hs, layer):
        f = ffn_w_down.shape[1]
        pad = (-f) % FFN_TILE if f > FFN_TILE else 0
        w_gu = ffn_w_gate_up[layer].astype(BF16).reshape(d, 2, f)
        w_gu = jnp.pad(w_gu, ((0, 0), (0, 0), (0, pad))).reshape(d, 2 * (f + pad))
        w_dn = ffn_w_down[layer].astype(BF16)
        g = row2(ffn_norm_g[layer])
        return [_mm_resid(_ffn_up(h, g, w_gu), w_dn, h, tm_pref=512, tn_pref=512, name="ffn_down")
                for h in hs]

    layer = 0
    for i in range(n_a):
        w_in = a_w_in[i]
        w_main = w_in[:, :2 * qk_tot + 2 * v_tot].astype(BF16)
        wg_t = w_in[:, 2 * qk_tot + 2 * v_tot:].T.astype(BF16)
        b_col = jnp.concatenate([a_b_i[i], a_b_f[i]]).reshape(2 * nh, 1).astype(F32)
        w_out = a_w_out[i].astype(BF16)
        g_in = row2(a_norm_g[i])
        g_out = row2(a_out_norm_g[i])
        state = (jnp.zeros((bsz, nh, dk, dv), F32), jnp.zeros((bsz, nh, 1, dk), F32),
                 jnp.zeros((bsz, nh, 8, 128), F32))
        new = [None, None]
        for s_idx in (1, 0):
            h = streams[s_idx]
            proj, gates_t = _in_proj(h, g_in, w_main, wg_t, b_col)
            mixed, state = _mlstm(proj, gates_t, g_out, state, bsz=bsz, chunk=chunks[s_idx])
            new[s_idx] = _mm_resid(mixed, w_out, h, tm_pref=1024, tn_pref=512, name="mlstm_out_proj")
        streams = swiglu(new, layer)
        layer += 1

    pos_real = (positions.astype(jnp.int32) + n_meta).reshape(bsz * seq)
    pos_meta = jnp.broadcast_to(jnp.arange(n_meta, dtype=jnp.int32)[None], (bsz, n_meta)).reshape(-1)
    tabs = [_rope_table(pos_real), _rope_table(pos_meta)]
    w_kv = _with_swapped_rope(kv_w_down, kv_lora).astype(BF16)
    w_up = kv_w_up.astype(BF16)
    gk_row = row2(_with_swapped_rope(k_norm_g, NOPE_DIM))
    g_kv = row2(kv_norm_g)
    g_lat = row2(kv_latent_norm_g)

    h_real, h_meta = streams
    ckv_m, kr_m = _kvq_down(h_meta, g_kv, w_kv, g_lat)
    k_meta, v_meta = _kv_up(ckv_m, kr_m, tabs[1], w_up, gk_row, heads=heads)
    k_real = v_real = None
    for j in range(depth - n_a):
        g_b = row2(b_norm_g[j])
        w_dq = b_w_dq[j].astype(BF16)
        g_ql = row2(b_q_latent_norm_g[j])
        ql = w_dq.shape[1]
        w_uq = _with_swapped_rope(b_w_uq[j].reshape(ql, heads, QK_HEAD_DIM), NOPE_DIM)
        w_uq = w_uq.reshape(ql, heads * HEAD_SLOT).astype(BF16)
        gq_row = row2(_with_swapped_rope(q_norm_g[j], NOPE_DIM))
        w_o = b_w_o[j].astype(BF16)
        if j == 0:
            ckv, kr, cq = _kvq_down(h_real, g_kv, w_kv, g_lat, (g_b, w_dq, g_ql))
            k_real, v_real = _kv_up(ckv, kr, tabs[0], w_up, gk_row, heads=heads)
        else:
            _, _, cq = _kvq_down(h_real, g_kv, w_kv, g_lat, (g_b, w_dq, g_ql))
        q = _q_up(cq, tabs[0], w_uq, gq_row, heads=heads)
        o = _attention(q, k_real, v_real, k_meta, v_meta, bsz=bsz, seq=seq, n_meta=n_meta)
        h_real = _mm_resid(o, w_o, h_real, tm_pref=512, tn_pref=512, name="mla_out_proj")
        (h_real,) = swiglu([h_real], layer)
        layer += 1
    return h_real.reshape(bsz, seq, d)
```

```python
import functools

import jax
import jax.numpy as jnp
from jax import lax
from jax.experimental import pallas as pl
from jax.experimental.pallas import tpu as pltpu

F32 = jnp.float32
BF16 = jnp.bfloat16

EPS = 1e-6
MLSTM_HEADS = 8
GATE_CAP = 15.0
NOPE_DIM = 128
ROPE_DIM = 64
V_HEAD_DIM = 128
QK_HEAD_DIM = NOPE_DIM + ROPE_DIM
HEAD_SLOT = 2 * NOPE_DIM
ROPE_THETA = 10000.0
LOG2_E = 1.4426950408889634
MLSTM_CHUNK = 256
FFN_TILE = 512
FFN_SUBTILE = 256
NORM_ROWS = 16

V7X_VMEM_BYTES = 64 * 1024 * 1024
VMEM_MARGIN_BYTES = 8 * 1024 * 1024
VMEM_CAP_BYTES = 58 * 1024 * 1024

NT_DIMS = (((1,), (1,)), ((), ()))
TN_DIMS = (((0,), (0,)), ((), ()))


def _params(semantics, buffer_bytes):
    limit = min(int(buffer_bytes) + VMEM_MARGIN_BYTES, VMEM_CAP_BYTES)
    return pltpu.CompilerParams(dimension_semantics=semantics, vmem_limit_bytes=limit)


def _tile(n, pref, align):
    if n <= pref:
        return n
    t = (pref // align) * align
    while t >= align:
        if n % t == 0:
            return t
        t -= align
    return n


def _rms_to_bf16(h_ref, gains, outs):
    tm = h_ref.shape[0]
    rc = NORM_ROWS if tm % NORM_ROWS == 0 else tm

    def body(c, carry):
        r0 = pl.multiple_of(c * rc, rc)
        x = h_ref[pl.ds(r0, rc), :]
        y = x * lax.rsqrt(jnp.mean(x * x, axis=-1, keepdims=True) + EPS)
        for g, o_ref in zip(gains, outs):
            o_ref[pl.ds(r0, rc), :] = (y * g).astype(BF16)
        return carry

    lax.fori_loop(0, tm // rc, body, 0)


def _sigmoid(x):
    return 1.0 / (1.0 + jnp.exp(-x))


def _in_proj_kernel(h_ref, g_ref, w_ref, wgt_ref, bcol_ref, o_ref, gt_ref, xn_ref):
    @pl.when(pl.program_id(1) == 0)
    def _():
        _rms_to_bf16(h_ref, [g_ref[...]], [xn_ref])
        z = lax.dot_general(wgt_ref[...], xn_ref[...], NT_DIMS, preferred_element_type=F32)
        z = z + bcol_ref[...]
        capped = GATE_CAP * jnp.tanh(z / GATE_CAP)
        logf = jnp.minimum(capped, 0.0) - jnp.log(1.0 + jnp.exp(-jnp.abs(capped)))
        row = lax.broadcasted_iota(jnp.int32, z.shape, 0)
        gt_ref[...] = jnp.where(row < MLSTM_HEADS, capped, logf)

    o_ref[...] = jnp.dot(xn_ref[...], w_ref[...], preferred_element_type=F32).astype(o_ref.dtype)


def _in_proj(h, g, w_main, wg_t, b_col):
    m, d = h.shape
    n = w_main.shape[1]
    tm = _tile(m, 1024, 16)
    tn = _tile(n, 1024, 128)
    ng = wg_t.shape[0]
    single = tm * d * 4 > 8 * 1024 * 1024
    h_spec = pl.BlockSpec((tm, d), lambda i, j: (i, 0),
                          pipeline_mode=pl.Buffered(1) if single else None)
    buf = (tm * d * 4 * (1 if single else 2) + tm * d * 2 + 2 * d * tn * 2 + 2 * tm * tn * 2
           + 2 * ng * d * 2 + 2 * ng * tm * 4 + tm * tn * 4)
    return pl.pallas_call(
        _in_proj_kernel,
        grid=(m // tm, n // tn),
        in_specs=[h_spec,
                  pl.BlockSpec((1, d), lambda i, j: (0, 0)),
                  pl.BlockSpec((d, tn), lambda i, j: (0, j)),
                  pl.BlockSpec((ng, d), lambda i, j: (0, 0)),
                  pl.BlockSpec((ng, 1), lambda i, j: (0, 0))],
        out_specs=[pl.BlockSpec((tm, tn), lambda i, j: (i, j)),
                   pl.BlockSpec((ng, tm), lambda i, j: (0, i))],
        out_shape=[jax.ShapeDtypeStruct((m, n), BF16),
                   jax.ShapeDtypeStruct((ng, m), F32)],
        scratch_shapes=[pltpu.VMEM((tm, d), BF16)],
        compiler_params=_params(("parallel", "arbitrary"), buf),
        name="mlstm_in_proj",
    )(h, g, w_main, wg_t, b_col)


def _mlstm_kernel(q_ref, k_ref, v_ref, og_ref, gc_ref, gr_ref, gout_ref, c0_ref, n0_ref, m0_ref,
                  o_ref, ct_ref, nt_ref, mt_ref, c_sc, n_sc, m_sc, *, k_scale):
    hd = pl.program_id(1)
    ci = pl.program_id(2)
    t = q_ref.shape[0]

    @pl.when(ci == 0)
    def _():
        c_sc[...] = c0_ref[0, 0]
        n_sc[...] = n0_ref[0, 0]
        m_sc[...] = m0_ref[0, 0]

    q = q_ref[...]
    k = k_ref[...] * k_scale
    v = v_ref[...]
    gc = gc_ref[0]
    gr = gr_ref[0]
    lane = lax.broadcasted_iota(jnp.int32, gc.shape, 1)
    sub = lax.broadcasted_iota(jnp.int32, gr.shape, 0)
    logi_c = jnp.sum(jnp.where(lane == hd, gc, 0.0), axis=1, keepdims=True)
    logf_c = jnp.sum(jnp.where(lane == hd + MLSTM_HEADS, gc, 0.0), axis=1, keepdims=True)
    logi_r = jnp.sum(jnp.where(sub == hd, gr, 0.0), axis=0, keepdims=True)
    logf_r = jnp.sum(jnp.where(sub == hd + MLSTM_HEADS, gr, 0.0), axis=0, keepdims=True)

    row = lax.broadcasted_iota(jnp.int32, (t, t), 0)
    col = lax.broadcasted_iota(jnp.int32, (t, t), 1)
    causal = row >= col
    b_c = jnp.sum(jnp.where(causal, logf_r, 0.0), axis=1, keepdims=True)
    b_r = jnp.sum(jnp.where(causal, 0.0, logf_c), axis=0, keepdims=True) + logf_r
    m_prev = m_sc[0:1, 0:1]
    c_prev = c_sc[...]
    n_prev = n_sc[...]

    d = jnp.where(causal, b_c - b_r + logi_r, -jnp.inf)
    inter = b_c + m_prev
    m_t = jnp.maximum(inter, jnp.max(d, axis=1, keepdims=True))
    w = jnp.exp(d - m_t)
    a = jnp.exp(inter - m_t)
    s = lax.dot_general(q, k, NT_DIMS, preferred_element_type=F32) * w
    num = (a * jnp.dot(q, c_prev.astype(BF16), preferred_element_type=F32)
           + jnp.dot(s.astype(BF16), v, preferred_element_type=F32))
    den = (a * jnp.sum(q.astype(F32) * n_prev, axis=1, keepdims=True)
           + jnp.sum(s, axis=1, keepdims=True))
    hval = num * (1.0 / jnp.maximum(jnp.abs(den), jnp.exp(-m_t)))

    ms = jnp.mean(hval * hval, axis=1, keepdims=True)
    hn = hval * lax.rsqrt(ms + EPS) * gout_ref[...]
    o_ref[...] = (hn * _sigmoid(og_ref[...].astype(F32))).astype(o_ref.dtype)

    b_last = jnp.sum(logf_r, axis=1, keepdims=True)
    g_c = b_last - b_c + logi_c
    m_new = jnp.maximum(b_last + m_prev, jnp.max(g_c, axis=0, keepdims=True))
    decay = jnp.exp(b_last + m_prev - m_new)
    kw = k.astype(F32) * jnp.exp(g_c - m_new)
    c_new = decay * c_prev + lax.dot_general(kw.astype(BF16), v, TN_DIMS, preferred_element_type=F32)
    n_new = decay * n_prev + jnp.sum(kw, axis=0, keepdims=True)
    c_sc[...] = c_new
    n_sc[...] = n_new
    m_sc[...] = jnp.broadcast_to(m_new, m_sc.shape)

    @pl.when(ci == pl.num_programs(2) - 1)
    def _():
        ct_ref[0, 0] = c_new
        nt_ref[0, 0] = n_new
        mt_ref[0, 0] = jnp.broadcast_to(m_new, m_sc.shape)


def _mlstm(proj, gates_t, g_out, state, *, bsz, chunk):
    m = proj.shape[0]
    c0, n0, m0 = state
    dk, dv = c0.shape[2], c0.shape[3]
    nh = MLSTM_HEADS
    nc = m // (bsz * chunk)
    g_row = gates_t.reshape(2 * nh, bsz * nc, chunk).transpose(1, 0, 2)
    g_col = g_row.transpose(0, 2, 1)
    kq = nh
    kv = (2 * nh * dk) // dv
    ko = kv + nh
    rowblk = lambda b, h, c: b * nc + c
    st_spec = lambda shp: pl.BlockSpec((1, 1) + shp, lambda b, h, c: (b, h, 0, 0))
    buf = 2 * (2 * chunk * dk * 2 + 2 * chunk * dv * 2 + chunk * dv * 2) + 8 * dk * dv * 4 + (1 << 20)
    out, ct, nt, mt = pl.pallas_call(
        functools.partial(_mlstm_kernel, k_scale=dk ** -0.5),
        grid=(bsz, nh, nc),
        in_specs=[pl.BlockSpec((chunk, dk), lambda b, h, c: (rowblk(b, h, c), h)),
                  pl.BlockSpec((chunk, dk), lambda b, h, c: (rowblk(b, h, c), kq + h)),
                  pl.BlockSpec((chunk, dv), lambda b, h, c: (rowblk(b, h, c), kv + h)),
                  pl.BlockSpec((chunk, dv), lambda b, h, c: (rowblk(b, h, c), ko + h)),
                  pl.BlockSpec((1, chunk, 2 * nh), lambda b, h, c: (rowblk(b, h, c), 0, 0)),
                  pl.BlockSpec((1, 2 * nh, chunk), lambda b, h, c: (rowblk(b, h, c), 0, 0)),
                  pl.BlockSpec((1, dv), lambda b, h, c: (0, h)),
                  st_spec((dk, dv)), st_spec((1, dk)), st_spec((8, 128))],
        out_specs=[pl.BlockSpec((chunk, dv), lambda b, h, c: (rowblk(b, h, c), h)),
                   st_spec((dk, dv)), st_spec((1, dk)), st_spec((8, 128))],
        out_shape=[jax.ShapeDtypeStruct((m, nh * dv), BF16),
                   jax.ShapeDtypeStruct(c0.shape, F32),
                   jax.ShapeDtypeStruct(n0.shape, F32),
                   jax.ShapeDtypeStruct(m0.shape, F32)],
        scratch_shapes=[pltpu.VMEM((dk, dv), F32), pltpu.VMEM((1, dk), F32), pltpu.VMEM((8, 128), F32)],
        compiler_params=_params(("parallel", "parallel", "arbitrary"), buf),
        name="mlstm_chunks",
    )(proj, proj, proj, proj, g_col, g_row, g_out, c0, n0, m0)
    return out, (ct, nt, mt)


def _mm_resid_kernel(x_ref, w_ref, r_ref, o_ref):
    o_ref[...] = r_ref[...] + jnp.dot(x_ref[...], w_ref[...], preferred_element_type=F32)


def _mm_resid(x, w, resid, *, tm_pref, tn_pref, name):
    m = x.shape[0]
    k, n = w.shape
    tm = _tile(m, tm_pref, 16)
    tn = _tile(n, tn_pref, 128)
    buf = 2 * (tm * k * 2 + k * tn * 2 + 2 * tm * tn * 4) + tm * tn * 4
    return pl.pallas_call(
        _mm_resid_kernel,
        grid=(m // tm, n // tn),
        in_specs=[pl.BlockSpec((tm, k), lambda i, j: (i, 0)),
                  pl.BlockSpec((k, tn), lambda i, j: (0, j)),
                  pl.BlockSpec((tm, tn), lambda i, j: (i, j))],
        out_specs=pl.BlockSpec((tm, tn), lambda i, j: (i, j)),
        out_shape=jax.ShapeDtypeStruct((m, n), F32),
        compiler_params=_params(("parallel", "parallel"), buf),
        name=name,
    )(x, w, resid)


def _ffn_up_kernel(h_ref, g_ref, wg_ref, wu_ref, o_ref, xn_ref):
    @pl.when(pl.program_id(1) == 0)
    def _():
        _rms_to_bf16(h_ref, [g_ref[...]], [xn_ref])

    x = xn_ref[...]
    sub = FFN_SUBTILE if o_ref.shape[1] % FFN_SUBTILE == 0 else o_ref.shape[1]
    for c in range(o_ref.shape[1] // sub):
        cs = slice(c * sub, (c + 1) * sub)
        gate = jnp.dot(x, wg_ref[:, cs], preferred_element_type=F32)
        up = jnp.dot(x, wu_ref[:, cs], preferred_element_type=F32)
        o_ref[:, cs] = (gate * _sigmoid(gate) * up).astype(o_ref.dtype)


def _ffn_up(h, g, w_gate, w_up):
    m, d = h.shape
    f = w_gate.shape[1]
    tm = _tile(m, 1024, 16)
    tn = _tile(f, FFN_TILE, 128)
    nj = f // tn
    single = tm * d * 4 > 8 * 1024 * 1024
    h_spec = pl.BlockSpec((tm, d), lambda i, j: (i, 0),
                          pipeline_mode=pl.Buffered(1) if single else None)
    buf = (tm * d * 4 * (1 if single else 2) + tm * d * 2 + 4 * d * tn * 2 + 2 * tm * tn * 2
           + 2 * tm * tn * 4)
    return pl.pallas_call(
        _ffn_up_kernel,
        grid=(m // tm, nj),
        in_specs=[h_spec,
                  pl.BlockSpec((1, d), lambda i, j: (0, 0)),
                  pl.BlockSpec((d, tn), lambda i, j: (0, j)),
                  pl.BlockSpec((d, tn), lambda i, j: (0, j))],
        out_specs=pl.BlockSpec((tm, tn), lambda i, j: (i, j)),
        out_shape=jax.ShapeDtypeStruct((m, f), BF16),
        scratch_shapes=[pltpu.VMEM((tm, d), BF16)],
        compiler_params=_params(("parallel", "arbitrary"), buf),
        name="ffn_up",
    )(h, g, w_gate, w_up)


def _kvq_down_kernel(*refs, with_q, kv_lora):
    if with_q:
        (h_ref, gkv_ref, gb_ref, wkv_ref, wdq_ref, glat_ref, gql_ref,
         ckv_ref, kr_ref, cq_ref, xa_ref, xb_ref) = refs
        _rms_to_bf16(h_ref, [gkv_ref[...], gb_ref[...]], [xa_ref, xb_ref])
    else:
        h_ref, gkv_ref, wkv_ref, glat_ref, ckv_ref, kr_ref, xa_ref = refs
        _rms_to_bf16(h_ref, [gkv_ref[...]], [xa_ref])

    a = jnp.dot(xa_ref[...], wkv_ref[...], preferred_element_type=F32)
    ckv = a[:, :kv_lora]
    ckv_ref[...] = (ckv * lax.rsqrt(jnp.mean(ckv * ckv, axis=-1, keepdims=True) + EPS)
                    * glat_ref[...]).astype(ckv_ref.dtype)
    kr_ref[...] = a[:, kv_lora:]
    if with_q:
        cq = jnp.dot(xb_ref[...], wdq_ref[...], preferred_element_type=F32)
        cq_ref[...] = (cq * lax.rsqrt(jnp.mean(cq * cq, axis=-1, keepdims=True) + EPS)
                       * gql_ref[...]).astype(cq_ref.dtype)


def _kvq_down(h, g_kv, w_kv, g_lat, q_parts=None):
    m, d = h.shape
    kv_lora = g_lat.shape[1]
    nkv = w_kv.shape[1]
    tm = _tile(m, 256, 16)
    row = lambda shp: pl.BlockSpec(shp, lambda i: (i, 0))
    full = lambda shp: pl.BlockSpec(shp, lambda i: (0, 0))
    with_q = q_parts is not None
    if with_q:
        g_b, w_dq, g_ql = q_parts
        nq = w_dq.shape[1]
        args = (h, g_kv, g_b, w_kv, w_dq, g_lat, g_ql)
        in_specs = [row((tm, d)), full((1, d)), full((1, d)), full((d, nkv)), full((d, nq)),
                    full((1, kv_lora)), full((1, nq))]
        out_specs = [row((tm, kv_lora)), row((tm, nkv - kv_lora)), row((tm, nq))]
        out_shape = [jax.ShapeDtypeStruct((m, kv_lora), BF16),
                     jax.ShapeDtypeStruct((m, nkv - kv_lora), F32),
                     jax.ShapeDtypeStruct((m, nq), BF16)]
        scratch = [pltpu.VMEM((tm, d), BF16), pltpu.VMEM((tm, d), BF16)]
        buf = 2 * tm * d * 4 + 2 * tm * d * 2 + 2 * d * (nkv + nq) * 2 + 2 * tm * (nkv + nq) * 4
    else:
        args = (h, g_kv, w_kv, g_lat)
        in_specs = [row((tm, d)), full((1, d)), full((d, nkv)), full((1, kv_lora))]
        out_specs = [row((tm, kv_lora)), row((tm, nkv - kv_lora))]
        out_shape = [jax.ShapeDtypeStruct((m, kv_lora), BF16),
                     jax.ShapeDtypeStruct((m, nkv - kv_lora), F32)]
        scratch = [pltpu.VMEM((tm, d), BF16)]
        buf = 2 * tm * d * 4 + tm * d * 2 + 2 * d * nkv * 2 + 2 * tm * nkv * 4
    return pl.pallas_call(
        functools.partial(_kvq_down_kernel, with_q=with_q, kv_lora=kv_lora),
        grid=(m // tm,),
        in_specs=in_specs, out_specs=out_specs, out_shape=out_shape, scratch_shapes=scratch,
        compiler_params=_params(("parallel",), buf),
        name="mla_down_q" if with_q else "mla_down_kv",
    )(*args)


def _fold_rope(y):
    return y + pltpu.roll(y, ROPE_DIM, 1)


def _kv_up_kernel(c_ref, kr_ref, tab_ref, w_ref, gk_ref, k_ref, v_ref, *, hb):
    c = c_ref[...]
    kr = kr_ref[...]
    gk = gk_ref[...]
    lane = lax.broadcasted_iota(jnp.int32, kr.shape, 1)
    low = lane < ROPE_DIM
    ssq_rope = jnp.sum(jnp.where(low, kr * kr, 0.0), axis=1, keepdims=True)
    pt = kr * gk[:, NOPE_DIM:] * tab_ref[...]
    for hd in range(hb):
        kvv = jnp.dot(c, w_ref[:, hd * HEAD_SLOT:(hd + 1) * HEAD_SLOT], preferred_element_type=F32)
        kn = kvv[:, :NOPE_DIM]
        r = lax.rsqrt((jnp.sum(kn * kn, axis=1, keepdims=True) + ssq_rope) * (1.0 / QK_HEAD_DIM) + EPS)
        k_ref[hd, :, 0:NOPE_DIM] = (kn * r * gk[:, :NOPE_DIM]).astype(k_ref.dtype)
        k_ref[hd, :, NOPE_DIM:] = jnp.where(low, _fold_rope(pt * r), 0.0).astype(k_ref.dtype)
        v_ref[hd] = kvv[:, NOPE_DIM:].astype(v_ref.dtype)


def _kv_up(ckv, kr, tab, w_up, gk_row, *, heads):
    m, kl = ckv.shape
    hb = 2
    tm = _tile(m, 1024, 16)
    buf = 2 * (tm * kl * 2 + 2 * tm * 128 * 4 + kl * hb * HEAD_SLOT * 2
               + hb * tm * HEAD_SLOT * 2 + hb * tm * V_HEAD_DIM * 2) + 4 * tm * HEAD_SLOT * 4
    return pl.pallas_call(
        functools.partial(_kv_up_kernel, hb=hb),
        grid=(m // tm, heads // hb),
        in_specs=[pl.BlockSpec((tm, kl), lambda i, j: (i, 0)),
                  pl.BlockSpec((tm, 2 * ROPE_DIM), lambda i, j: (i, 0)),
                  pl.BlockSpec((tm, 2 * ROPE_DIM), lambda i, j: (i, 0)),
                  pl.BlockSpec((kl, hb * HEAD_SLOT), lambda i, j: (0, j)),
                  pl.BlockSpec((1, HEAD_SLOT), lambda i, j: (0, 0))],
        out_specs=[pl.BlockSpec((hb, tm, HEAD_SLOT), lambda i, j: (j, i, 0)),
                   pl.BlockSpec((hb, tm, V_HEAD_DIM), lambda i, j: (j, i, 0))],
        out_shape=[jax.ShapeDtypeStruct((heads, m, HEAD_SLOT), BF16),
                   jax.ShapeDtypeStruct((heads, m, V_HEAD_DIM), BF16)],
        compiler_params=_params(("parallel", "parallel"), buf),
        name="mla_kv_up",
    )(ckv, kr, tab, w_up, gk_row)


def _q_up_kernel(c_ref, tab_ref, w_ref, gq_ref, q_ref, *, hb, scale):
    c = c_ref[...]
    gq = gq_ref[...]
    tab = tab_ref[...]
    lane = lax.broadcasted_iota(jnp.int32, tab.shape, 1)
    low = lane < ROPE_DIM
    for hd in range(hb):
        qv = jnp.dot(c, w_ref[:, hd * HEAD_SLOT:(hd + 1) * HEAD_SLOT], preferred_element_type=F32)
        qn = qv[:, :NOPE_DIM]
        qr = qv[:, NOPE_DIM:]
        ssq = (jnp.sum(qn * qn, axis=1, keepdims=True)
               + jnp.sum(jnp.where(low, qr * qr, 0.0), axis=1, keepdims=True))
        r = lax.rsqrt(ssq * (1.0 / QK_HEAD_DIM) + EPS) * scale
        q_ref[hd, :, 0:NOPE_DIM] = (qn * r * gq[:, :NOPE_DIM]).astype(q_ref.dtype)
        q_ref[hd, :, NOPE_DIM:] = _fold_rope(qr * gq[:, NOPE_DIM:] * tab * r).astype(q_ref.dtype)


def _q_up(cq, tab, w_uq, gq_row, *, heads):
    m, ql = cq.shape
    hb = 2
    tm = _tile(m, 1024, 16)
    buf = 2 * (tm * ql * 2 + tm * 128 * 4 + ql * hb * HEAD_SLOT * 2 + hb * tm * HEAD_SLOT * 2) \
        + 4 * tm * HEAD_SLOT * 4
    return pl.pallas_call(
        functools.partial(_q_up_kernel, hb=hb, scale=QK_HEAD_DIM ** -0.5 * LOG2_E),
        grid=(m // tm, heads // hb),
        in_specs=[pl.BlockSpec((tm, ql), lambda i, j: (i, 0)),
                  pl.BlockSpec((tm, 2 * ROPE_DIM), lambda i, j: (i, 0)),
                  pl.BlockSpec((ql, hb * HEAD_SLOT), lambda i, j: (0, j)),
                  pl.BlockSpec((1, HEAD_SLOT), lambda i, j: (0, 0))],
        out_specs=pl.BlockSpec((hb, tm, HEAD_SLOT), lambda i, j: (j, i, 0)),
        out_shape=jax.ShapeDtypeStruct((heads, m, HEAD_SLOT), BF16),
        compiler_params=_params(("parallel", "parallel"), buf),
        name="mla_q_up",
    )(cq, tab, w_uq, gq_row)


def _attn_kernel(q_ref, k_ref, v_ref, km_ref, vm_ref, o_ref, va_sc, vma_sc, m_sc, acc_sc, *, hb, tq):
    qt = pl.program_id(2)
    lanes = V_HEAD_DIM

    @pl.when(qt == 0)
    def _():
        for hd in range(hb):
            va_sc[hd, :, 0:lanes] = v_ref[hd]
            va_sc[hd, :, lanes:] = jnp.ones((va_sc.shape[1], lanes), va_sc.dtype)
            vma_sc[hd, :, 0:lanes] = vm_ref[hd]
            vma_sc[hd, :, lanes:] = jnp.ones((vma_sc.shape[1], lanes), vma_sc.dtype)

    def update(hd, r0, masked):
        k = k_ref[hd, pl.ds(r0, tq), :]
        s = lax.dot_general(q_ref[hd], k, NT_DIMS, preferred_element_type=F32)
        if masked:
            row = lax.broadcasted_iota(jnp.int32, (tq, tq), 0)
            col = lax.broadcasted_iota(jnp.int32, (tq, tq), 1)
            s = jnp.where(row >= col, s, -jnp.inf)
        m_old = m_sc[hd]
        m_new = jnp.maximum(m_old, jnp.max(s, axis=1, keepdims=True))
        alpha = jnp.exp2(m_old - m_new)
        p = jnp.exp2(s - jnp.concatenate([m_new] * (tq // lanes), axis=1))
        pv = jnp.dot(p.astype(BF16), va_sc[hd, pl.ds(r0, tq), :], preferred_element_type=F32)
        acc_sc[hd] = jnp.concatenate([alpha, alpha], axis=1) * acc_sc[hd] + pv
        m_sc[hd] = m_new

    for hd in range(hb):
        s0 = lax.dot_general(q_ref[hd], km_ref[hd], NT_DIMS, preferred_element_type=F32)
        m0 = jnp.max(s0, axis=1, keepdims=True)
        p0 = jnp.exp2(s0 - m0)
        m_sc[hd] = jnp.broadcast_to(m0, (tq, lanes))
        acc_sc[hd] = jnp.dot(p0.astype(BF16), vma_sc[hd], preferred_element_type=F32)

    def body(kc, carry):
        for half in range(2):
            r0 = pl.multiple_of((2 * kc + half) * tq, tq)
            for hd in range(hb):
                update(hd, r0, False)
        return carry

    lax.fori_loop(0, qt // 2, body, 0)

    @pl.when(qt % 2 == 1)
    def _():
        r0 = pl.multiple_of((qt - 1) * tq, tq)
        for hd in range(hb):
            update(hd, r0, False)

    r0 = pl.multiple_of(qt * tq, tq)
    for hd in range(hb):
        update(hd, r0, True)
    for hd in range(hb):
        acc = acc_sc[hd]
        o_ref[:, hd * lanes:(hd + 1) * lanes] = (acc[:, :lanes] * (1.0 / acc[:, lanes:])).astype(o_ref.dtype)


def _attention(q, k, v, km, vm, *, bsz, seq, n_meta):
    heads = q.shape[0]
    hb = 4
    tq = _tile(seq, 512, 128)
    nq = seq // tq
    buf = 2 * (hb * tq * HEAD_SLOT * 2 + hb * seq * (HEAD_SLOT + V_HEAD_DIM) * 2
               + hb * n_meta * (HEAD_SLOT + V_HEAD_DIM) * 2 + tq * hb * V_HEAD_DIM * 2) \
        + hb * (seq + n_meta) * 2 * V_HEAD_DIM * 2 + hb * tq * 3 * 128 * 4 + 6 * tq * tq * 4
    return pl.pallas_call(
        functools.partial(_attn_kernel, hb=hb, tq=tq),
        grid=(bsz, heads // hb, nq),
        in_specs=[pl.BlockSpec((hb, tq, HEAD_SLOT), lambda b, h, t: (h, b * nq + t, 0)),
                  pl.BlockSpec((hb, seq, HEAD_SLOT), lambda b, h, t: (h, b, 0)),
                  pl.BlockSpec((hb, seq, V_HEAD_DIM), lambda b, h, t: (h, b, 0)),
                  pl.BlockSpec((hb, n_meta, HEAD_SLOT), lambda b, h, t: (h, b, 0)),
                  pl.BlockSpec((hb, n_meta, V_HEAD_DIM), lambda b, h, t: (h, b, 0))],
        out_specs=pl.BlockSpec((tq, hb * V_HEAD_DIM), lambda b, h, t: (b * nq + t, h)),
        out_shape=jax.ShapeDtypeStruct((bsz * seq, heads * V_HEAD_DIM), BF16),
        scratch_shapes=[pltpu.VMEM((hb, seq, 2 * V_HEAD_DIM), BF16),
                        pltpu.VMEM((hb, n_meta, 2 * V_HEAD_DIM), BF16),
                        pltpu.VMEM((hb, tq, V_HEAD_DIM), F32),
                        pltpu.VMEM((hb, tq, 2 * V_HEAD_DIM), F32)],
        compiler_params=_params(("parallel", "parallel", "arbitrary"), buf),
        name="mla_attention",
    )(q, k, v, km, vm)


def _rope_table(pos):
    half = ROPE_DIM // 2
    inv = 1.0 / (ROPE_THETA ** (jnp.arange(0, ROPE_DIM, 2, dtype=F32) / ROPE_DIM))
    ang = pos.astype(F32)[:, None] * inv
    c, s = jnp.cos(ang), jnp.sin(ang)
    del half
    return jnp.concatenate([c, c, -s, s], axis=-1)


def _with_swapped_rope(w, lead):
    half = ROPE_DIM // 2
    return jnp.concatenate([w[..., :lead + ROPE_DIM], w[..., lead + half:lead + ROPE_DIM],
                            w[..., lead:lead + half]], axis=-1)


def kernel(x, positions, meta_tokens, a_norm_g, a_w_in, a_b_i, a_b_f, a_out_norm_g, a_w_out,
           ffn_norm_g, ffn_w_gate_up, ffn_w_down, kv_norm_g, kv_w_down, kv_latent_norm_g,
           kv_w_up, k_norm_g, b_norm_g, b_w_dq, b_q_latent_norm_g, b_w_uq, q_norm_g, b_w_o):
    bsz, seq, d = x.shape
    n_meta = meta_tokens.shape[0]
    n_a = a_w_in.shape[0]
    depth = ffn_w_gate_up.shape[0]
    nh = MLSTM_HEADS
    v_tot = a_w_out.shape[1]
    qk_tot = (a_w_in.shape[2] - 2 * v_tot - 2 * nh) // 2
    dk, dv = qk_tot // nh, v_tot // nh
    kv_lora = kv_latent_norm_g.shape[0]
    heads = kv_w_up.shape[1] // (NOPE_DIM + V_HEAD_DIM)
    row2 = lambda g: g.reshape(1, -1).astype(F32)

    streams = [x.reshape(bsz * seq, d),
               jnp.broadcast_to(meta_tokens[None].astype(x.dtype), (bsz, n_meta, d)).reshape(bsz * n_meta, d)]
    chunks = [_tile(seq, MLSTM_CHUNK, 128), n_meta]

    def swiglu(hs, layer):
        f = ffn_w_down.shape[1]
        pad = (-f) % FFN_TILE if f > FFN_TILE else 0
        w_gate = jnp.pad(ffn_w_gate_up[layer][:, :f].astype(BF16), ((0, 0), (0, pad)))
        w_up = jnp.pad(ffn_w_gate_up[layer][:, f:].astype(BF16), ((0, 0), (0, pad)))
        w_dn = ffn_w_down[layer].astype(BF16)
        g = row2(ffn_norm_g[layer])
        return [_mm_resid(_ffn_up(h, g, w_gate, w_up), w_dn, h, tm_pref=512, tn_pref=512, name="ffn_down")
                for h in hs]

    layer = 0
    for i in range(n_a):
        w_in = a_w_in[i]
        w_main = w_in[:, :2 * qk_tot + 2 * v_tot].astype(BF16)
        wg_t = w_in[:, 2 * qk_tot + 2 * v_tot:].T.astype(BF16)
        b_col = jnp.concatenate([a_b_i[i], a_b_f[i]]).reshape(2 * nh, 1).astype(F32)
        w_out = a_w_out[i].astype(BF16)
        g_in = row2(a_norm_g[i])
        g_out = row2(a_out_norm_g[i])
        state = (jnp.zeros((bsz, nh, dk, dv), F32), jnp.zeros((bsz, nh, 1, dk), F32),
                 jnp.zeros((bsz, nh, 8, 128), F32))
        new = [None, None]
        for s_idx in (1, 0):
            h = streams[s_idx]
            proj, gates_t = _in_proj(h, g_in, w_main, wg_t, b_col)
            mixed, state = _mlstm(proj, gates_t, g_out, state, bsz=bsz, chunk=chunks[s_idx])
            new[s_idx] = _mm_resid(mixed, w_out, h, tm_pref=1024, tn_pref=512, name="mlstm_out_proj")
        streams = swiglu(new, layer)
        layer += 1

    pos_real = (positions.astype(jnp.int32) + n_meta).reshape(bsz * seq)
    pos_meta = jnp.broadcast_to(jnp.arange(n_meta, dtype=jnp.int32)[None], (bsz, n_meta)).reshape(-1)
    tabs = [_rope_table(pos_real), _rope_table(pos_meta)]
    w_kv = _with_swapped_rope(kv_w_down, kv_lora).astype(BF16)
    w_up = kv_w_up.astype(BF16)
    gk_row = row2(_with_swapped_rope(k_norm_g, NOPE_DIM))
    g_kv = row2(kv_norm_g)
    g_lat = row2(kv_latent_norm_g)

    h_real, h_meta = streams
    ckv_m, kr_m = _kvq_down(h_meta, g_kv, w_kv, g_lat)
    k_meta, v_meta = _kv_up(ckv_m, kr_m, tabs[1], w_up, gk_row, heads=heads)
    k_real = v_real = None
    for j in range(depth - n_a):
        g_b = row2(b_norm_g[j])
        w_dq = b_w_dq[j].astype(BF16)
        g_ql = row2(b_q_latent_norm_g[j])
        ql = w_dq.shape[1]
        w_uq = _with_swapped_rope(b_w_uq[j].reshape(ql, heads, QK_HEAD_DIM), NOPE_DIM)
        w_uq = w_uq.reshape(ql, heads * HEAD_SLOT).astype(BF16)
        gq_row = row2(_with_swapped_rope(q_norm_g[j], NOPE_DIM))
        w_o = b_w_o[j].astype(BF16)
        if j == 0:
            ckv, kr, cq = _kvq_down(h_real, g_kv, w_kv, g_lat, (g_b, w_dq, g_ql))
            k_real, v_real = _kv_up(ckv, kr, tabs[0], w_up, gk_row, heads=heads)
        else:
            _, _, cq = _kvq_down(h_real, g_kv, w_kv, g_lat, (g_b, w_dq, g_ql))
        q = _q_up(cq, tabs[0], w_uq, gq_row, heads=heads)
        o = _attention(q, k_real, v_real, k_meta, v_meta, bsz=bsz, seq=seq, n_meta=n_meta)
        h_real = _mm_resid(o, w_o, h_real, tm_pref=512, tn_pref=512, name="mla_out_proj")
        (h_real,) = swiglu([h_real], layer)
        layer += 1
    return h_real.reshape(bsz, seq, d)
```
